```python
import math, functools
import jax, jax.numpy as jnp
from jax import lax
import numpy as np

D_MODEL = 1024
BATCH = 1
SEQ = 16384
DEPTH = 1
DEC_BATCH = 128
DEC_SEQ = 8
PAST_LEN = 8192
PAGE_SIZE = 128

M_HEADS = 4
M_HEAD_DIM = D_MODEL // 8
M_WIDTH = M_HEADS * M_HEAD_DIM
M_CHUNK = 64
A_HEADS = 4
A_HEAD_DIM = D_MODEL // 8
A_WIDTH = A_HEADS * A_HEAD_DIM
MOBA_BLOCK = 256
MOBA_TOPK = 3
Q_BLOCK = 128
REL_BUCKETS = 32
REL_MAX_DIST = 4096
D_FF = 4 * D_MODEL
N_COND = 6
NORM_EPS = 1e-6
NEG_INF = -1e30
IN_SIZES = (M_WIDTH, M_WIDTH, M_WIDTH, M_WIDTH, M_HEADS, M_HEADS, A_WIDTH, A_WIDTH, A_WIDTH, D_MODEL, D_MODEL)
IN_OFFSETS = tuple(sum(IN_SIZES[:i + 1]) for i in range(len(IN_SIZES) - 1))
IN_TOTAL = sum(IN_SIZES)

kernel_name = 'hybrid_mlstm_moba_adaln_decoder_step'


def rmsnorm(x, g):
    xf = x.astype(jnp.float32)
    y = xf * lax.rsqrt(jnp.mean(xf * xf, axis=-1, keepdims=True) + NORM_EPS)
    return (y * g.astype(jnp.float32)).astype(x.dtype)


def ada_modulation(c, w_ada, b_ada):
    h = jax.nn.silu(c) @ w_ada + b_ada
    return jnp.split(h[:, None, :], N_COND, axis=-1)


def rel_bucket(dist):
    n = jnp.maximum(dist, 0)
    max_exact = REL_BUCKETS // 2
    nf = jnp.maximum(n, 1).astype(jnp.float32)
    large = max_exact + (jnp.log(nf / max_exact) / math.log(REL_MAX_DIST / max_exact)
                         * (REL_BUCKETS - max_exact)).astype(jnp.int32)
    large = jnp.minimum(large, REL_BUCKETS - 1)
    return jnp.where(n < max_exact, n, large)


def mlstm_chunkwise(q, k, v, i_pre, log_f, C0, n0, m0):
    B, T, H, d = q.shape
    L = math.gcd(T, M_CHUNK)
    nc = T // L

    def to_chunks(a):
        return jnp.moveaxis(a.reshape((B, nc, L) + a.shape[2:]), 1, 0)

    causal = jnp.tril(jnp.ones((L, L), dtype=bool))

    def step(carry, inp):
        C, n, m = carry
        qb, kb, vb, ib, fb = inp
        b = jnp.cumsum(fb, axis=1)
        dmat = b[:, :, None, :] - b[:, None, :, :] + ib[:, None, :, :]
        dmat = jnp.where(causal[None, :, :, None], dmat, -jnp.inf)
        a = b + m[:, None, :]
        m_t = jnp.maximum(a, jnp.max(dmat, axis=2))
        w_intra = jnp.exp(dmat - m_t[:, :, None, :])
        w_inter = jnp.exp(a - m_t)
        s = jnp.einsum('bthd,bshd->btsh', qb, kb) * w_intra
        num = jnp.einsum('btsh,bshd->bthd', s, vb) + w_inter[..., None] * jnp.einsum('bthk,bhkv->bthv', qb, C)
        den = jnp.sum(s, axis=2) + w_inter * jnp.einsum('bthk,bhk->bth', qb, n)
        h = num / jnp.maximum(jnp.abs(den), jnp.exp(-m_t))[..., None]
        m_new = m_t[:, -1]
        decay = jnp.exp(a[:, -1] - m_new)
        w_s = jnp.exp(b[:, -1:, :] - b + ib - m_new[:, None, :])
        C_new = decay[..., None, None] * C + jnp.einsum('bsh,bshk,bshv->bhkv', w_s, kb, vb)
        n_new = decay[..., None] * n + jnp.einsum('bsh,bshk->bhk', w_s, kb)
        return (C_new, n_new, m_new), h

    (C, n, m), hs = lax.scan(step, (C0, n0, m0), tuple(map(to_chunks, (q, k, v, i_pre, log_f))))
    return jnp.moveaxis(hs, 0, 1).reshape(B, T, H, d), C, n, m


def moba_prompt(q, k, v, rel_bias):
    B, S, H, d = q.shape
    NB = -(-S // MOBA_BLOCK)
    pad = NB * MOBA_BLOCK - S
    kp = jnp.pad(k, ((0, 0), (0, pad), (0, 0), (0, 0)))
    vp = jnp.pad(v, ((0, 0), (0, pad), (0, 0), (0, 0)))
    kbh = jnp.transpose(kp.reshape(B, NB, MOBA_BLOCK, H, d), (0, 3, 1, 2, 4))
    vbh = jnp.transpose(vp.reshape(B, NB, MOBA_BLOCK, H, d), (0, 3, 1, 2, 4))
    kmean = jnp.mean(kbh, axis=3)
    ksel = min(MOBA_TOPK, NB - 1)
    scale = d ** -0.5
    nq = S // Q_BLOCK
    q_blocks = jnp.moveaxis(q.reshape(B, nq, Q_BLOCK, H, d), 1, 0)
    bi = jnp.arange(B)[:, None, None, None]
    hi = jnp.arange(H)[None, None, :, None]
    blk_ar = jnp.arange(MOBA_BLOCK)

    def one_query_block(args):
        i, qb = args
        q0 = i * Q_BLOCK
        qpos = q0 + jnp.arange(Q_BLOCK)
        j = q0 // MOBA_BLOCK
        start = j * MOBA_BLOCK
        k_own = lax.dynamic_slice_in_dim(kp, start, MOBA_BLOCK, axis=1)
        v_own = lax.dynamic_slice_in_dim(vp, start, MOBA_BLOCK, axis=1)
        kpos = start + blk_ar
        bias_own = jnp.transpose(rel_bias[rel_bucket(qpos[:, None] - kpos[None, :])], (0, 2, 1))
        lg_own = jnp.einsum('bqhd,bkhd->bqhk', qb, k_own) * scale + bias_own
        lg_own = jnp.where((kpos[None, :] <= qpos[:, None])[:, None, :], lg_own, NEG_INF)
        if ksel == 0:
            return jnp.einsum('bqhk,bkhd->bqhd', jax.nn.softmax(lg_own, axis=-1), v_own)
        sc = jnp.einsum('bqhd,bhnd->bqhn', qb, kmean)
        sc = jnp.where(jnp.arange(NB) < j, sc, NEG_INF)
        _, idx = lax.top_k(sc, ksel)
        k_sel = kbh[bi, hi, idx]
        v_sel = vbh[bi, hi, idx]
        kpos_sel = idx[..., None] * MOBA_BLOCK + blk_ar
        bias_sel = rel_bias[rel_bucket(qpos[None, :, None, None, None] - kpos_sel), hi[..., None]]
        lg_sel = jnp.einsum('bqhd,bqhskd->bqhsk', qb, k_sel) * scale + bias_sel
        lg_sel = jnp.where((idx < j)[..., None], lg_sel, NEG_INF)
        n_sel = ksel * MOBA_BLOCK
        logits = jnp.concatenate([lg_sel.reshape(lg_sel.shape[:3] + (n_sel,)), lg_own], axis=-1)
        p = jax.nn.softmax(logits, axis=-1)
        p_sel = p[..., :n_sel].reshape(lg_sel.shape)
        return (jnp.einsum('bqhsk,bqhskd->bqhd', p_sel, v_sel)
                + jnp.einsum('bqhk,bkhd->bqhd', p[..., n_sel:], v_own))

    out = lax.map(one_query_block, (jnp.arange(nq), q_blocks))
    return jnp.moveaxis(out, 0, 1).reshape(B, S, H, d)


def moba_sample(q, k_new, v_new, cache_k, cache_v, page_table, rel_bias):
    DB, T, H, d = q.shape
    n_pages = page_table.shape[1]
    P = n_pages * PAGE_SIZE
    ppb = MOBA_BLOCK // PAGE_SIZE
    npb = n_pages // ppb
    ksel = min(MOBA_TOPK, npb)
    scale = d ** -0.5
    f32 = jnp.float32
    n_own = min(ppb - 1, n_pages)
    if n_own > 0:
        own_pages = page_table[:, n_pages - n_own:]
        k_own = jnp.concatenate([cache_k[own_pages].reshape(DB, n_own * PAGE_SIZE, H, d).astype(f32), k_new], axis=1)
        v_own = jnp.concatenate([cache_v[own_pages].reshape(DB, n_own * PAGE_SIZE, H, d).astype(f32), v_new], axis=1)
    else:
        k_own, v_own = k_new, v_new
    own_pos = (P - n_own * PAGE_SIZE) + jnp.arange(n_own * PAGE_SIZE + T)
    if ksel > 0:
        page_sum = jnp.sum(cache_k.astype(f32), axis=1)
        blk_mean = page_sum[page_table[:, :npb * ppb]].reshape(DB, npb, ppb, H, d).sum(2) / MOBA_BLOCK
    bi = jnp.arange(DB)[:, None, None, None]
    hi = jnp.arange(H)[None, :, None, None]
    rows = jnp.arange(PAGE_SIZE)
    blk_ar = jnp.arange(MOBA_BLOCK)

    def one_token(args):
        t_pos, qt = args
        j = t_pos // MOBA_BLOCK
        start = j * MOBA_BLOCK
        bias_own = rel_bias[rel_bucket(t_pos - own_pos)].T
        lg_own = jnp.einsum('bhd,brhd->bhr', qt, k_own) * scale + bias_own
        lg_own = jnp.where(((own_pos >= start) & (own_pos <= t_pos))[None, None, :], lg_own, NEG_INF)
        if ksel == 0:
            return jnp.einsum('bhr,brhd->bhd', jax.nn.softmax(lg_own, axis=-1), v_own)
        sc = jnp.einsum('bhd,bnhd->bhn', qt, blk_mean)
        sc = jnp.where(jnp.arange(npb) < j, sc, NEG_INF)
        _, idx = lax.top_k(sc, ksel)
        phys = page_table[bi, idx[..., None] * ppb + jnp.arange(ppb)]
        hh = hi[..., None]
        k_sel = cache_k[phys[..., None], rows, hh].reshape(DB, H, ksel, MOBA_BLOCK, d).astype(f32)
        v_sel = cache_v[phys[..., None], rows, hh].reshape(DB, H, ksel, MOBA_BLOCK, d).astype(f32)
        kpos_sel = idx[..., None] * MOBA_BLOCK + blk_ar
        bias_sel = rel_bias[rel_bucket(t_pos - kpos_sel), hi]
        lg_sel = jnp.einsum('bhd,bhskd->bhsk', qt, k_sel) * scale + bias_sel
        lg_sel = jnp.where((idx < j)[..., None], lg_sel, NEG_INF)
        n_sel = ksel * MOBA_BLOCK
        logits = jnp.concatenate([lg_sel.reshape(DB, H, n_sel), lg_own], axis=-1)
        p = jax.nn.softmax(logits, axis=-1)
        p_sel = p[..., :n_sel].reshape(lg_sel.shape)
        return (jnp.einsum('bhsk,bhskd->bhd', p_sel, v_sel)
                + jnp.einsum('bhr,brhd->bhd', p[..., n_sel:], v_own))

    out = lax.map(one_token, (P + jnp.arange(T), jnp.moveaxis(q, 1, 0)))
    return jnp.moveaxis(out, 0, 1)


def decoder_layer(x, c, C0, n0, m0, moba_fn, w_ada, b_ada, g_norm_mix, w_in, b_igate, b_fgate,
                  g_mhead, g_qnorm, g_knorm, w_branch_m, w_branch_a, w_out, g_norm_ffn, w_ff_up, w_ff_down):
    f32 = jnp.float32
    B, T, _ = x.shape
    shift1, scale1, gate1, shift2, scale2, gate2 = ada_modulation(c, w_ada, b_ada)
    h = rmsnorm(x, g_norm_mix) * (1 + scale1) + shift1
    mq, mk, mv, mo, mi, mf, aq, ak, av, gm, ga = jnp.split(h @ w_in, IN_OFFSETS, axis=-1)

    def m_heads(z):
        return z.reshape(B, T, M_HEADS, M_HEAD_DIM).astype(f32)

    def a_heads(z):
        return z.reshape(B, T, A_HEADS, A_HEAD_DIM).astype(f32)

    i_pre = mi.astype(f32) + b_igate.astype(f32)
    log_f = jax.nn.log_sigmoid(mf.astype(f32) + b_fgate.astype(f32))
    hm, C, n, m = mlstm_chunkwise(m_heads(mq), m_heads(mk) * M_HEAD_DIM ** -0.5, m_heads(mv), i_pre, log_f,
                                  C0.astype(f32), n0.astype(f32), m0.astype(f32))
    hm = rmsnorm(hm, g_mhead) * jax.nn.sigmoid(m_heads(mo))
    qa = rmsnorm(a_heads(aq), g_qnorm)
    ka = rmsnorm(a_heads(ak), g_knorm)
    va = a_heads(av)
    ha = moba_fn(qa, ka, va)
    mix = (jax.nn.sigmoid(gm) * (hm.reshape(B, T, M_WIDTH).astype(x.dtype) @ w_branch_m)
           + jax.nn.sigmoid(ga) * (ha.reshape(B, T, A_WIDTH).astype(x.dtype) @ w_branch_a))
    x = x + gate1 * (mix @ w_out)
    h2 = rmsnorm(x, g_norm_ffn) * (1 + scale2) + shift2
    x = x + gate2 * (jnp.square(jax.nn.relu(h2 @ w_ff_up)) @ w_ff_down)
    return x, ka, va, C, n, m


def setup_inputs(seed: int = 0) -> dict:
    key = jax.random.key(seed)
    ks = jax.random.split(key, 26)
    f32 = jnp.float32
    n_pages = PAST_LEN // PAGE_SIZE
    n_used = DEC_BATCH * n_pages
    n_pool = n_used + n_used // 4

    def nrm(k, shape, s):
        return s * jax.random.normal(k, shape, f32)

    page_table = jax.random.permutation(ks[0], n_pool)[:n_used].reshape(DEC_BATCH, n_pages).astype(jnp.int32)
    return {
        'x_prompt': nrm(ks[1], (BATCH, SEQ, D_MODEL), 1.0),
        'x_sample': nrm(ks[2], (DEC_BATCH, DEC_SEQ, D_MODEL), 1.0),
        'cache_k': nrm(ks[3], (DEPTH, n_pool, PAGE_SIZE, A_HEADS, A_HEAD_DIM), 1.0),
        'cache_v': nrm(ks[4], (DEPTH, n_pool, PAGE_SIZE, A_HEADS, A_HEAD_DIM), 1.0),
        'state_C': nrm(ks[5], (DEPTH, DEC_BATCH, M_HEADS, M_HEAD_DIM, M_HEAD_DIM), 0.1),
        'state_n': nrm(ks[6], (DEPTH, DEC_BATCH, M_HEADS, M_HEAD_DIM), 0.1),
        'state_m': nrm(ks[7], (DEPTH, DEC_BATCH, M_HEADS), 0.5),
        'page_table': page_table,
        'c_prompt': nrm(ks[8], (BATCH, D_MODEL), 1.0),
        'c_sample': nrm(ks[9], (DEC_BATCH, D_MODEL), 1.0),
        'rel_bias': nrm(ks[10], (REL_BUCKETS, A_HEADS), 0.2),
        'w_ada': nrm(ks[11], (DEPTH, D_MODEL, N_COND * D_MODEL), 0.5 * D_MODEL ** -0.5),
        'b_ada': nrm(ks[12], (DEPTH, N_COND * D_MODEL), 0.01),
        'g_norm_mix': 1.0 + nrm(ks[13], (DEPTH, D_MODEL), 0.02),
        'w_in': nrm(ks[14], (DEPTH, D_MODEL, IN_TOTAL), D_MODEL ** -0.5),
        'b_igate': nrm(ks[15], (DEPTH, M_HEADS), 0.1),
        'b_fgate': 3.0 + nrm(ks[16], (DEPTH, M_HEADS), 0.5),
        'g_mhead': 1.0 + nrm(ks[17], (DEPTH, M_HEADS, M_HEAD_DIM), 0.02),
        'g_qnorm': 1.0 + nrm(ks[18], (DEPTH, A_HEADS, A_HEAD_DIM), 0.02),
        'g_knorm': 1.0 + nrm(ks[19], (DEPTH, A_HEADS, A_HEAD_DIM), 0.02),
        'w_branch_m': nrm(ks[20], (DEPTH, M_WIDTH, D_MODEL), M_WIDTH ** -0.5),
        'w_branch_a': nrm(ks[21], (DEPTH, A_WIDTH, D_MODEL), A_WIDTH ** -0.5),
        'w_out': nrm(ks[22], (DEPTH, D_MODEL, D_MODEL), D_MODEL ** -0.5),
        'g_norm_ffn': 1.0 + nrm(ks[23], (DEPTH, D_MODEL), 0.02),
        'w_ff_up': nrm(ks[24], (DEPTH, D_MODEL, D_FF), D_MODEL ** -0.5),
        'w_ff_down': nrm(ks[25], (DEPTH, D_FF, D_MODEL), D_FF ** -0.5),
    }


def reference(x_prompt, x_sample, cache_k, cache_v, state_C, state_n, state_m, page_table, c_prompt, c_sample,
              rel_bias, w_ada, b_ada, g_norm_mix, w_in, b_igate, b_fgate, g_mhead, g_qnorm, g_knorm,
              w_branch_m, w_branch_a, w_out, g_norm_ffn, w_ff_up, w_ff_down):
    f32 = jnp.float32
    B = x_prompt.shape[0]
    yp, ys = x_prompt, x_sample
    kp_l, vp_l, Cp_l, np_l, mp_l = [], [], [], [], []
    ks_l, vs_l, Cs_l, ns_l, ms_l = [], [], [], [], []
    for l in range(DEPTH):
        lw = (w_ada[l], b_ada[l], g_norm_mix[l], w_in[l], b_igate[l], b_fgate[l], g_mhead[l], g_qnorm[l],
              g_knorm[l], w_branch_m[l], w_branch_a[l], w_out[l], g_norm_ffn[l], w_ff_up[l], w_ff_down[l])
        C0 = jnp.zeros((B, M_HEADS, M_HEAD_DIM, M_HEAD_DIM), f32)
        n0 = jnp.zeros((B, M_HEADS, M_HEAD_DIM), f32)
        m0 = jnp.zeros((B, M_HEADS), f32)
        prompt_moba = functools.partial(moba_prompt, rel_bias=rel_bias)
        yp, kp, vp, Cp, np_, mp = decoder_layer(yp, c_prompt, C0, n0, m0, prompt_moba, *lw)
        sample_moba = functools.partial(moba_sample, cache_k=cache_k[l], cache_v=cache_v[l],
                                        page_table=page_table, rel_bias=rel_bias)
        ys, kn, vn, Cs, ns, ms = decoder_layer(ys, c_sample, state_C[l], state_n[l], state_m[l], sample_moba, *lw)
        kp_l.append(kp.astype(cache_k.dtype)); vp_l.append(vp.astype(cache_v.dtype))
        Cp_l.append(Cp.astype(state_C.dtype)); np_l.append(np_.astype(state_n.dtype)); mp_l.append(mp.astype(state_m.dtype))
        ks_l.append(kn.astype(cache_k.dtype)); vs_l.append(vn.astype(cache_v.dtype))
        Cs_l.append(Cs.astype(state_C.dtype)); ns_l.append(ns.astype(state_n.dtype)); ms_l.append(ms.astype(state_m.dtype))
    k_prompt = jnp.stack(kp_l); v_prompt = jnp.stack(vp_l)
    C_prompt = jnp.stack(Cp_l); n_prompt = jnp.stack(np_l); m_prompt = jnp.stack(mp_l)
    k_sample = jnp.stack(ks_l); v_sample = jnp.stack(vs_l)
    C_sample = jnp.stack(Cs_l); n_sample = jnp.stack(ns_l); m_sample = jnp.stack(ms_l)
    return (yp, ys, k_prompt, v_prompt, C_prompt, n_prompt, m_prompt, k_sample, v_sample, C_sample, n_sample, m_sample)
```

```python
import functools
import math

import jax
import jax.numpy as jnp
from jax import lax
from jax.experimental import pallas as pl
from jax.experimental.pallas import tpu as pltpu

F32 = jnp.float32
BF16 = jnp.bfloat16
HIGHEST = lax.Precision.HIGHEST

N_HEADS = 4
HEAD_DIM = 128
WIDTH = N_HEADS * HEAD_DIM
MOBA_BLOCK = 256
MOBA_TOPK = 3
PAGE_SIZE = 128
REL_BUCKETS = 32
REL_MAX_DIST = 4096
N_COND = 6
NORM_EPS = 1e-6
NEG_INF = -1e30
GATE_LANES = 128
GATE_ROWS = 16

V7X_VMEM_BYTES = 64 * 1024 * 1024
VMEM_HEADROOM_BYTES = 8 * 1024 * 1024
TOKEN_TILE = 256
assert TOKEN_TILE == MOBA_BLOCK, "feature-major projection tiles double as MoBA key/query blocks"
PAGES_PER_STEP = 8

REL_SAT = 3072


def _vmem_limit(nbytes):
    return int(min(V7X_VMEM_BYTES - VMEM_HEADROOM_BYTES, nbytes + VMEM_HEADROOM_BYTES))


def _nbytes(shape, dtype):
    return math.prod(shape) * jnp.dtype(dtype).itemsize


def _params(sem, nbytes):
    return pltpu.CompilerParams(dimension_semantics=sem, vmem_limit_bytes=_vmem_limit(nbytes))


def _const_spec(shape):
    nd = len(shape)
    return pl.BlockSpec(shape, lambda *_: (0,) * nd, pipeline_mode=pl.Buffered(1))


def _dot(a, b):
    return jnp.dot(a, b, preferred_element_type=F32)


def _dot_nt(a, b):
    return lax.dot_general(a, b, (((1,), (1,)), ((), ())), preferred_element_type=F32)


def _log_sigmoid(x):
    return -(jnp.maximum(-x, 0.0) + jnp.log1p(jnp.exp(-jnp.abs(x))))


def _ada_kernel(c_ref, w_ref, b_ref, o_ref):
    c = c_ref[...]
    s = (c * jax.nn.sigmoid(c)).astype(BF16)
    o_ref[...] = _dot(s, w_ref[...].astype(BF16)) + b_ref[...]


def _ada(c_all, w_ada, b_ada):
    rows, d = c_all.shape
    n_out = w_ada.shape[1]
    tn = d
    nbytes = 2 * (_nbytes((rows, d), F32) + _nbytes((d, tn), F32) + _nbytes((rows, tn), F32))
    return pl.pallas_call(
        _ada_kernel,
        grid=(n_out // tn,),
        in_specs=[pl.BlockSpec((rows, d), lambda j: (0, 0)),
                  pl.BlockSpec((d, tn), lambda j: (0, j)),
                  pl.BlockSpec((1, tn), lambda j: (0, j))],
        out_specs=pl.BlockSpec((rows, tn), lambda j: (0, j)),
        out_shape=jax.ShapeDtypeStruct((rows, n_out), F32),
        compiler_params=_params(("arbitrary",), nbytes),
        name="ada",
    )(c_all, w_ada, b_ada.reshape(1, n_out))


def _head_norm_rows(z, g):
    outs = []
    for h in range(N_HEADS):
        blk = z[:, h * HEAD_DIM:(h + 1) * HEAD_DIM]
        ms = jnp.mean(blk * blk, axis=-1, keepdims=True)
        outs.append(blk * lax.rsqrt(ms + NORM_EPS) * g[:, h * HEAD_DIM:(h + 1) * HEAD_DIM])
    return jnp.concatenate(outs, axis=-1)


def _head_norm_cols(z, g):
    outs = []
    for h in range(N_HEADS):
        blk = z[h * HEAD_DIM:(h + 1) * HEAD_DIM, :]
        ms = jnp.mean(blk * blk, axis=0, keepdims=True)
        outs.append(blk * lax.rsqrt(ms + NORM_EPS) * g[h * HEAD_DIM:(h + 1) * HEAD_DIM, :])
    return jnp.concatenate(outs, axis=0)


def _in_proj_kernel(plan, x_ref, shift_ref, scale_ref, g_ref, *refs):
    n_groups = len(plan)
    w_refs = refs[:n_groups]
    n_aux = sum(1 for p in plan if p["aux"])
    aux_refs = refs[n_groups:n_groups + n_aux]
    out_refs = refs[n_groups + n_aux:]

    x = x_ref[...]
    y = x * lax.rsqrt(jnp.mean(x * x, axis=-1, keepdims=True) + NORM_EPS) * g_ref[...]
    hb = (y * (1.0 + scale_ref[...]) + shift_ref[...]).astype(BF16)

    ai = 0
    oi = 0
    for p, w_ref in zip(plan, w_refs):
        z = _dot_nt(w_ref[...], hb) if p["trans"] else _dot(hb, w_ref[...])
        aux = None
        if p["aux"]:
            aux = aux_refs[ai][...]
            ai += 1
        epi = p["epi"]
        if epi == "scale":
            z = z * p["const"]
        elif epi == "sigmoid":
            z = jax.nn.sigmoid(z)
        elif epi == "norm_row":
            z = _head_norm_rows(z, aux)
        elif epi == "norm_col":
            z = _head_norm_cols(z, aux)
        elif epi == "gate_col":
            pre = z + aux
            lane = lax.broadcasted_iota(jnp.int32, pre.shape, 1)
            z = jnp.where(lane < N_HEADS, pre, _log_sigmoid(pre))
        elif epi == "gate_row":
            pre = z + aux
            row = lax.broadcasted_iota(jnp.int32, pre.shape, 0)
            z = jnp.where(row < N_HEADS, pre, _log_sigmoid(pre))
        for dt in p["outs"]:
            if p["trans"]:
                out_refs[oi][0] = z.astype(dt)
            else:
                out_refs[oi][...] = z.astype(dt)
            oi += 1


def _in_proj(x, shift, scale, g_norm, plan, tm):
    t, d = x.shape
    per_tok = shift.shape[0] != 1
    mod_spec = (pl.BlockSpec((tm, d), lambda i: (i, 0)) if per_tok
                else pl.BlockSpec((1, d), lambda i: (0, 0)))
    in_specs = [pl.BlockSpec((tm, d), lambda i: (i, 0)), mod_spec, mod_spec, _const_spec((1, d))]
    args = [x, shift, scale, g_norm.reshape(1, d)]
    nbytes = 2 * _nbytes((tm, d), F32) * (3 if per_tok else 1)
    for p in plan:
        in_specs.append(_const_spec(p["w"].shape))
        args.append(p["w"])
        nbytes += _nbytes(p["w"].shape, p["w"].dtype)
    for p in plan:
        if p["aux"]:
            in_specs.append(_const_spec(p["auxv"].shape))
            args.append(p["auxv"])
    out_specs, out_shapes = [], []
    for p in plan:
        n = p["w"].shape[0] if p["trans"] else p["w"].shape[1]
        for dt in p["outs"]:
            if p["trans"]:
                out_specs.append(pl.BlockSpec((1, n, tm), lambda i: (i, 0, 0)))
                out_shapes.append(jax.ShapeDtypeStruct((t // tm, n, tm), dt))
            else:
                out_specs.append(pl.BlockSpec((tm, n), lambda i: (i, 0)))
                out_shapes.append(jax.ShapeDtypeStruct((t, n), dt))
            nbytes += 2 * _nbytes((tm, n), dt)
    plan_static = tuple({k: v for k, v in p.items() if k not in ("w", "auxv")} for p in plan)
    return pl.pallas_call(
        functools.partial(_in_proj_kernel, plan_static),
        grid=(t // tm,),
        in_specs=in_specs,
        out_specs=out_specs,
        out_shape=out_shapes,
        compiler_params=_params(("parallel",), nbytes + 4 * _nbytes((tm, 1024), F32)),
        name="in_proj",
    )(*args)


def _group(w, trans, epi="none", const=None, auxv=None, outs=(F32,)):
    w = (w.T if trans else w).astype(BF16)
    return dict(w=w, trans=trans, epi=epi, const=const, aux=auxv is not None, auxv=auxv, outs=tuple(outs))


def _split_w_in(w_in):
    sizes = (WIDTH, WIDTH, WIDTH, WIDTH, N_HEADS, N_HEADS, WIDTH, WIDTH, WIDTH, w_in.shape[0], w_in.shape[0])
    names = ("mq", "mk", "mv", "mo", "mi", "mf", "aq", "ak", "av", "gm", "ga")
    out, off = {}, 0
    for n, s in zip(names, sizes):
        out[n] = w_in[:, off:off + s]
        off += s
    assert off == w_in.shape[1]
    return out


def _gate_groups(cols, b_igate, b_fgate):
    w_gate = jnp.concatenate([cols["mi"], cols["mf"]], axis=1)
    bias = jnp.concatenate([b_igate, b_fgate]).astype(F32)
    pad_to = lambda a, n, axis: jnp.pad(a, [(0, n - a.shape[i]) if i == axis else (0, 0) for i in range(a.ndim)])
    return (_group(pad_to(w_gate, GATE_LANES, 1), False, "gate_col", auxv=pad_to(bias.reshape(1, -1), GATE_LANES, 1)),
            _group(pad_to(w_gate, GATE_ROWS, 1), True, "gate_row", auxv=pad_to(bias.reshape(-1, 1), GATE_ROWS, 0)))


def _mlstm_prompt_kernel(qT_ref, k_ref, vT_ref, oT_ref, gcol_ref, grow_ref, gm_ref,
                         h_ref, c_out_ref, n_out_ref, m_out_ref, c_s, n_s, m_s):
    step = pl.program_id(0)
    L = k_ref.shape[0]

    @pl.when(step == 0)
    def _():
        c_s[...] = jnp.zeros_like(c_s)
        n_s[...] = jnp.zeros_like(n_s)
        m_s[...] = jnp.zeros_like(m_s)

    r = lax.broadcasted_iota(jnp.int32, (L, L), 0)
    c = lax.broadcasted_iota(jnp.int32, (L, L), 1)
    tril = (c <= r).astype(F32)
    triu = (r <= c)
    gcol = gcol_ref[...]
    grow = grow_ref[0]
    bcol_all = jnp.dot(tril, gcol, precision=HIGHEST, preferred_element_type=F32)
    brow_all = jnp.dot(grow, triu.astype(F32), precision=HIGHEST, preferred_element_type=F32)

    for h in range(N_HEADS):
        sl = slice(h * HEAD_DIM, (h + 1) * HEAD_DIM)
        qTb = qT_ref[0, sl, :].astype(BF16)
        k = k_ref[:, sl]
        vTb = vT_ref[0, sl, :].astype(BF16)
        i_col = gcol[:, h:h + 1]
        b_col = bcol_all[:, N_HEADS + h:N_HEADS + h + 1]
        b_row = brow_all[N_HEADS + h:N_HEADS + h + 1, :]
        m_prev = m_s[h][0:1, 0:1]
        ct = c_s[h]
        n_row = n_s[h][0:1, :]

        g_col = i_col - b_col
        dT = jnp.where(triu, g_col + b_row, -jnp.inf)
        m_intra = jnp.max(dT, axis=0, keepdims=True)
        a_row = b_row + m_prev
        m_t = jnp.maximum(a_row, m_intra)
        wT = jnp.exp(dT - m_t)
        w_inter = jnp.exp(a_row - m_t)

        sT = _dot(k.astype(BF16), qTb)
        pT = sT * wT
        den = jnp.sum(pT, axis=0, keepdims=True)
        numT = _dot(vTb, pT.astype(BF16))
        cq = _dot(ct.astype(BF16), qTb)
        nq = _dot(jnp.broadcast_to(n_row, (8, HEAD_DIM)).astype(BF16), qTb)[0:1, :]
        numT = numT + w_inter * cq
        den = den + w_inter * nq
        hT = numT / jnp.maximum(jnp.abs(den), jnp.exp(-m_t))

        ms = jnp.mean(hT * hT, axis=0, keepdims=True)
        yT = hT * lax.rsqrt(ms + NORM_EPS) * gm_ref[sl, :] * oT_ref[0, sl, :]
        h_ref[:, sl] = yT.T

        m_new = m_t[:, L - 1:L]
        b_last = b_row[:, L - 1:L]
        decay = jnp.exp(b_last + m_prev - m_new)
        w_s = jnp.exp(b_last + g_col - m_new)
        kw = k * w_s
        c_new = decay * ct + _dot(vTb, kw.astype(BF16))
        n_new = decay * n_row + jnp.sum(kw, axis=0, keepdims=True)
        c_s[h] = c_new
        n_s[h] = jnp.broadcast_to(n_new, (8, HEAD_DIM))
        m_s[h] = jnp.broadcast_to(m_new, (8, HEAD_DIM))

    @pl.when(step == pl.num_programs(0) - 1)
    def _():
        for h in range(N_HEADS):
            c_out_ref[h] = c_s[h].T
        n_out_ref[...] = n_s[...]
        m_out_ref[...] = m_s[...]


def _mlstm_prompt(qT, k, vT, oT, gcol, grow, g_mhead):
    t = k.shape[0]
    L = qT.shape[2]
    assert t % L == 0 and qT.shape[0] * L == t
    blk_bytes = 3 * _nbytes((WIDTH, L), F32) + 2 * _nbytes((L, WIDTH), F32) + _nbytes((L, GATE_LANES), F32)
    state = jax.ShapeDtypeStruct((N_HEADS, 8, HEAD_DIM), F32)
    outs = pl.pallas_call(
        _mlstm_prompt_kernel,
        grid=(t // L,),
        in_specs=[pl.BlockSpec((1, WIDTH, L), lambda i: (i, 0, 0)),
                  pl.BlockSpec((L, WIDTH), lambda i: (i, 0)),
                  pl.BlockSpec((1, WIDTH, L), lambda i: (i, 0, 0)),
                  pl.BlockSpec((1, WIDTH, L), lambda i: (i, 0, 0)),
                  pl.BlockSpec((L, GATE_LANES), lambda i: (i, 0)),
                  pl.BlockSpec((1, GATE_ROWS, L), lambda i: (i, 0, 0)),
                  _const_spec((WIDTH, 1))],
        out_specs=[pl.BlockSpec((L, WIDTH), lambda i: (i, 0)),
                   pl.BlockSpec((N_HEADS, HEAD_DIM, HEAD_DIM), lambda i: (0, 0, 0)),
                   pl.BlockSpec((N_HEADS, 8, HEAD_DIM), lambda i: (0, 0, 0)),
                   pl.BlockSpec((N_HEADS, 8, HEAD_DIM), lambda i: (0, 0, 0))],
        out_shape=[jax.ShapeDtypeStruct((t, WIDTH), F32),
                   jax.ShapeDtypeStruct((N_HEADS, HEAD_DIM, HEAD_DIM), F32), state, state],
        scratch_shapes=[pltpu.VMEM((N_HEADS, HEAD_DIM, HEAD_DIM), F32),
                        pltpu.VMEM((N_HEADS, 8, HEAD_DIM), F32),
                        pltpu.VMEM((N_HEADS, 8, HEAD_DIM), F32)],
        compiler_params=_params(("arbitrary",), 2 * blk_bytes + 16 * _nbytes((L, L), F32)),
        name="mlstm_prompt",
    )(qT, k, vT, oT, gcol, grow, g_mhead.reshape(WIDTH, 1))
    hm, c_fin, n_fin, m_fin = outs
    return hm, c_fin, n_fin[:, 0, :], m_fin[:, 0, 0]


def _mlstm_sample_kernel(q_ref, k_ref, kT_ref, v_ref, o_ref, gcol_ref, grow_ref, c0_ref, n0_ref, m0_ref,
                         gm_ref, h_ref, c_out_ref, n_out_ref, m_out_ref):
    L = q_ref.shape[1]
    r = lax.broadcasted_iota(jnp.int32, (L, L), 0)
    c = lax.broadcasted_iota(jnp.int32, (L, L), 1)
    causal = c <= r
    gcol = gcol_ref[0]
    grow = grow_ref[0]
    bcol_all = jnp.dot(causal.astype(F32), gcol, precision=HIGHEST, preferred_element_type=F32)
    brow_all = jnp.dot(grow, (r <= c).astype(F32), precision=HIGHEST, preferred_element_type=F32)
    m0 = m0_ref[0]

    for h in range(N_HEADS):
        sl = slice(h * HEAD_DIM, (h + 1) * HEAD_DIM)
        q = q_ref[0][:, sl]
        k = k_ref[0][:, sl]
        kT = kT_ref[0][sl, :]
        v = v_ref[0][:, sl]
        i_col = gcol[:, h:h + 1]
        i_row = grow[h:h + 1, :]
        b_col = bcol_all[:, N_HEADS + h:N_HEADS + h + 1]
        b_row = brow_all[N_HEADS + h:N_HEADS + h + 1, :]
        m_prev = m0[h:h + 1, 0:1]
        c_old = c0_ref[0, h]
        n_old = n0_ref[0][h:h + 1, :]

        dm = jnp.where(causal, b_col - b_row + i_row, -jnp.inf)
        m_intra = jnp.max(dm, axis=1, keepdims=True)
        a_col = b_col + m_prev
        m_t = jnp.maximum(a_col, m_intra)
        w = jnp.exp(dm - m_t)
        w_inter = jnp.exp(a_col - m_t)

        s = _dot_nt(q, k) * w
        num = _dot(s, v) + w_inter * _dot(q.astype(BF16), c_old.astype(BF16))
        qn = jnp.sum(q * n_old, axis=1, keepdims=True)
        den = jnp.sum(s, axis=1, keepdims=True) + w_inter * qn
        hh = num / jnp.maximum(jnp.abs(den), jnp.exp(-m_t))
        ms = jnp.mean(hh * hh, axis=-1, keepdims=True)
        h_ref[0, :, sl] = hh * lax.rsqrt(ms + NORM_EPS) * gm_ref[:, sl] * o_ref[0][:, sl]

        m_new = m_t[L - 1:L, :]
        b_last = b_col[L - 1:L, :]
        decay = jnp.exp(b_last + m_prev - m_new)
        ws_row = jnp.exp(b_last - b_row + i_row - m_new)
        ws_col = jnp.exp(b_last - b_col + i_col - m_new)
        c_out_ref[0, h] = decay * c_old + _dot(kT * ws_row, v)
        n_out_ref[0, h:h + 1, :] = decay * n_old + jnp.sum(k * ws_col, axis=0, keepdims=True)
        m_out_ref[0, h:h + 1, :] = jnp.broadcast_to(m_new, (1, HEAD_DIM))


def _mlstm_sample(q, k, kT, v, o, gcol, grow, c0, n0, m0, g_mhead):
    db, L, _ = q.shape
    seq = lambda *shape: pl.BlockSpec((1,) + shape, lambda b: (b,) + (0,) * len(shape))
    m0_rep = jnp.broadcast_to(m0[:, :, None], (db, N_HEADS, HEAD_DIM))
    nbytes = 2 * 2 * _nbytes((N_HEADS, HEAD_DIM, HEAD_DIM), F32) + 16 * _nbytes((8, WIDTH), F32)
    outs = pl.pallas_call(
        _mlstm_sample_kernel,
        grid=(db,),
        in_specs=[seq(L, WIDTH), seq(L, WIDTH), seq(WIDTH, L), seq(L, WIDTH), seq(L, WIDTH),
                  seq(L, GATE_LANES), seq(GATE_ROWS, L),
                  seq(N_HEADS, HEAD_DIM, HEAD_DIM), seq(N_HEADS, HEAD_DIM), seq(N_HEADS, HEAD_DIM),
                  _const_spec((1, WIDTH))],
        out_specs=[seq(L, WIDTH), seq(N_HEADS, HEAD_DIM, HEAD_DIM), seq(N_HEADS, HEAD_DIM),
                   seq(N_HEADS, HEAD_DIM)],
        out_shape=[jax.ShapeDtypeStruct((db, L, WIDTH), F32),
                   jax.ShapeDtypeStruct((db, N_HEADS, HEAD_DIM, HEAD_DIM), F32),
                   jax.ShapeDtypeStruct((db, N_HEADS, HEAD_DIM), F32),
                   jax.ShapeDtypeStruct((db, N_HEADS, HEAD_DIM), F32)],
        compiler_params=_params(("parallel",), nbytes),
        name="mlstm_sample",
    )(q, k, kT, v, o, gcol, grow, c0, n0, m0_rep, g_mhead.reshape(1, WIDTH))
    hm, c_new, n_new, m_new = outs
    return hm, c_new, n_new, m_new[:, :, 0]


def _rel_bias_from_dist(dist, rb_ref, h):
    max_exact = REL_BUCKETS // 2
    n = jnp.maximum(dist, 0)
    nf = jnp.maximum(n, 1).astype(F32)
    large = max_exact + (jnp.log(nf / max_exact) / math.log(REL_MAX_DIST / max_exact)
                         * (REL_BUCKETS - max_exact)).astype(jnp.int32)
    large = jnp.minimum(large, REL_BUCKETS - 1)
    bucket = jnp.where(n < max_exact, n, large)
    out = jnp.zeros(dist.shape, F32)
    for b in range(REL_BUCKETS):
        out = jnp.where(bucket == b, rb_ref[b * N_HEADS + h], out)
    return out


def _prompt_bias_kernel(rb_ref, o_ref):
    delta = pl.program_id(0)
    h = pl.program_id(1)
    s = lax.broadcasted_iota(jnp.int32, (MOBA_BLOCK, MOBA_BLOCK), 0)
    t = lax.broadcasted_iota(jnp.int32, (MOBA_BLOCK, MOBA_BLOCK), 1)
    dist = delta * MOBA_BLOCK + t - s
    bias = _rel_bias_from_dist(dist, rb_ref, h) - rb_ref[(REL_BUCKETS - 1) * N_HEADS + h]
    o_ref[0, 0] = jnp.where(dist >= 0, bias, NEG_INF)


def _near_blocks():
    return -(-(REL_SAT + MOBA_BLOCK - 1) // MOBA_BLOCK)


def _prompt_bias(rel_bias):
    nd = _near_blocks()
    return pl.pallas_call(
        _prompt_bias_kernel,
        grid=(nd, N_HEADS),
        in_specs=[pl.BlockSpec(memory_space=pltpu.SMEM)],
        out_specs=pl.BlockSpec((1, 1, MOBA_BLOCK, MOBA_BLOCK), lambda d, h: (d, h, 0, 0)),
        out_shape=jax.ShapeDtypeStruct((nd, N_HEADS, MOBA_BLOCK, MOBA_BLOCK), F32),
        compiler_params=_params(("parallel", "parallel"), 4 * _nbytes((MOBA_BLOCK, MOBA_BLOCK), F32)),
        name="prompt_bias",
    )(rel_bias.reshape(-1))


def _block_mean_kernel(k_ref, o_ref):
    nb = o_ref.shape[0]
    for b in range(nb):
        o_ref[b:b + 1, :] = jnp.mean(k_ref[b * MOBA_BLOCK:(b + 1) * MOBA_BLOCK, :], axis=0, keepdims=True)


def _block_mean(k, nb_step=8):
    t, w = k.shape
    nb = t // MOBA_BLOCK
    nb_step = min(nb_step, nb)
    assert nb % nb_step == 0
    rows = nb_step * MOBA_BLOCK
    return pl.pallas_call(
        _block_mean_kernel,
        grid=(nb // nb_step,),
        in_specs=[pl.BlockSpec((rows, w), lambda i: (i, 0))],
        out_specs=pl.BlockSpec((nb_step, w), lambda i: (i, 0)),
        out_shape=jax.ShapeDtypeStruct((nb, w), F32),
        compiler_params=_params(("parallel",), 2 * _nbytes((rows, w), F32)),
        name="block_mean",
    )(k)


def _top_mask(sc, valid, ksel, axis):
    n = sc.shape[axis]
    iota = lax.broadcasted_iota(jnp.int32, sc.shape, axis)
    cur = jnp.where(valid, sc, NEG_INF)
    sel = jnp.zeros(sc.shape, F32)
    for _ in range(ksel):
        mx = jnp.max(cur, axis=axis, keepdims=True)
        idx = jnp.min(jnp.where(cur == mx, iota, n), axis=axis, keepdims=True)
        hit = iota == idx
        sel = jnp.where(hit, 1.0, sel)
        cur = jnp.where(hit, -jnp.inf, cur)
    return jnp.where((sel > 0.0) & valid, 0.0, NEG_INF)


def _prompt_select_kernel(ksel, kmean_ref, qT_ref, o_ref):
    tq = qT_ref.shape[2]
    nb = kmean_ref.shape[0]
    q0 = pl.program_id(0) * tq
    qpos = q0 + lax.broadcasted_iota(jnp.int32, (nb, tq), 1)
    blk = lax.broadcasted_iota(jnp.int32, (nb, tq), 0)
    valid = blk < qpos // MOBA_BLOCK
    for h in range(N_HEADS):
        sl = slice(h * HEAD_DIM, (h + 1) * HEAD_DIM)
        sc = jnp.dot(kmean_ref[:, sl], qT_ref[0, sl, :], precision=HIGHEST, preferred_element_type=F32)
        o_ref[h] = _top_mask(sc, valid, ksel, 0)


def _prompt_select(kmean, qT):
    nb = kmean.shape[0]
    nt, _, tq = qT.shape
    ksel = min(MOBA_TOPK, nb - 1)
    return pl.pallas_call(
        functools.partial(_prompt_select_kernel, ksel),
        grid=(nt,),
        in_specs=[_const_spec((nb, WIDTH)), pl.BlockSpec((1, WIDTH, tq), lambda i: (i, 0, 0))],
        out_specs=pl.BlockSpec((N_HEADS, nb, tq), lambda i: (0, 0, i)),
        out_shape=jax.ShapeDtypeStruct((N_HEADS, nb, nt * tq), F32),
        compiler_params=_params(("parallel",), 4 * _nbytes((WIDTH, tq), F32)),
        name="prompt_select",
    )(kmean, qT)


def _moba_prompt_kernel(nd, qT_ref, k_ref, vT_ref, mask_ref, bias_ref, o_ref):
    i = pl.program_id(1)
    scale = HEAD_DIM ** -0.5
    qTb = qT_ref[0].astype(BF16)

    def block(j):
        start = pl.multiple_of(j * MOBA_BLOCK, MOBA_BLOCK)
        kb = k_ref[pl.ds(start, MOBA_BLOCK), :]
        vTb = vT_ref[j]
        return _dot(kb, qTb) * scale, vTb

    def update(carry, sT, vTb):
        m, l, acc = carry
        m_new = jnp.maximum(m, jnp.max(sT, axis=0, keepdims=True))
        alpha = jnp.exp(m - m_new)
        p = jnp.exp(sT - m_new)
        l = alpha * l + jnp.sum(p, axis=0, keepdims=True)
        acc = alpha * acc + _dot(vTb, p.astype(BF16))
        return m_new, l, acc

    sT, vTb = block(i)
    sT = sT + bias_ref[0, 0]
    m = jnp.max(sT, axis=0, keepdims=True)
    p = jnp.exp(sT - m)
    carry = (m, jnp.sum(p, axis=0, keepdims=True), _dot(vTb, p.astype(BF16)))

    first_near = jnp.maximum(i - (nd - 1), 0)

    def near(j, carry):
        sT, vTb = block(j)
        sT = sT + bias_ref[i - j, 0] + mask_ref[0, pl.ds(j, 1), :]
        return update(carry, sT, vTb)

    def far(j, carry):
        sT, vTb = block(j)
        return update(carry, sT + mask_ref[0, pl.ds(j, 1), :], vTb)

    carry = lax.fori_loop(first_near, i, near, carry)
    m, l, acc = lax.fori_loop(0, first_near, far, carry)
    o_ref[...] = (acc / l).T


def _moba_prompt(qT, k_bf, vT_bf, mask, bias):
    t = k_bf.shape[0]
    nb = t // MOBA_BLOCK
    nd = bias.shape[0]
    assert qT.shape == (nb, WIDTH, MOBA_BLOCK) and vT_bf.shape == qT.shape
    nbytes = (2 * (_nbytes((t, HEAD_DIM), BF16) * 2 + _nbytes((nd, MOBA_BLOCK, MOBA_BLOCK), F32)
                   + _nbytes((nb, MOBA_BLOCK), F32)) + 12 * _nbytes((MOBA_BLOCK, MOBA_BLOCK), F32))
    return pl.pallas_call(
        functools.partial(_moba_prompt_kernel, nd),
        grid=(N_HEADS, nb),
        in_specs=[pl.BlockSpec((1, HEAD_DIM, MOBA_BLOCK), lambda h, i: (i, h, 0)),
                  pl.BlockSpec((t, HEAD_DIM), lambda h, i: (0, h)),
                  pl.BlockSpec((nb, HEAD_DIM, MOBA_BLOCK), lambda h, i: (0, h, 0)),
                  pl.BlockSpec((1, nb, MOBA_BLOCK), lambda h, i: (h, 0, i)),
                  pl.BlockSpec((nd, 1, MOBA_BLOCK, MOBA_BLOCK), lambda h, i: (0, h, 0, 0))],
        out_specs=pl.BlockSpec((MOBA_BLOCK, HEAD_DIM), lambda h, i: (i, h)),
        out_shape=jax.ShapeDtypeStruct((t, WIDTH), F32),
        compiler_params=_params(("arbitrary", "arbitrary"), nbytes),
        name="moba_prompt",
    )(qT, k_bf, vT_bf, mask, bias)


def _sample_bias_kernel(past_len, n_new, rb_ref, past_ref, page_ref, new_ref):
    t_rows = past_ref.shape[1] // N_HEADS
    page = pl.program_id(0)
    for h in range(N_HEADS):
        rows = slice(h * t_rows, (h + 1) * t_rows)
        tpos = past_len + lax.broadcasted_iota(jnp.int32, (t_rows, PAGE_SIZE), 0)
        kpos = page * PAGE_SIZE + lax.broadcasted_iota(jnp.int32, (t_rows, PAGE_SIZE), 1)
        past_ref[0, rows, :] = _rel_bias_from_dist(tpos - kpos, rb_ref, h)

        tpos = past_len + lax.broadcasted_iota(jnp.int32, (t_rows, PAGE_SIZE), 0)
        kpos = past_len - PAGE_SIZE + lax.broadcasted_iota(jnp.int32, (t_rows, PAGE_SIZE), 1)
        ok = (kpos >= (tpos // MOBA_BLOCK) * MOBA_BLOCK) & (kpos <= tpos)
        page_ref[rows, :] = jnp.where(ok, _rel_bias_from_dist(tpos - kpos, rb_ref, h), NEG_INF)

        width = new_ref.shape[1]
        tpos = past_len + lax.broadcasted_iota(jnp.int32, (t_rows, width), 0)
        kcol = lax.broadcasted_iota(jnp.int32, (t_rows, width), 1)
        kpos = past_len + kcol
        ok = (kcol < n_new) & (kpos >= (tpos // MOBA_BLOCK) * MOBA_BLOCK) & (kpos <= tpos)
        new_ref[rows, :] = jnp.where(ok, _rel_bias_from_dist(tpos - kpos, rb_ref, h), NEG_INF)


def _sample_bias(rel_bias, past_len, t_new, new_pad):
    rows = N_HEADS * t_new
    n_pages = past_len // PAGE_SIZE
    return pl.pallas_call(
        functools.partial(_sample_bias_kernel, past_len, t_new),
        grid=(n_pages,),
        in_specs=[pl.BlockSpec(memory_space=pltpu.SMEM)],
        out_specs=[pl.BlockSpec((1, rows, PAGE_SIZE), lambda p: (p, 0, 0)),
                   pl.BlockSpec((rows, PAGE_SIZE), lambda p: (0, 0)),
                   pl.BlockSpec((rows, new_pad), lambda p: (0, 0))],
        out_shape=[jax.ShapeDtypeStruct((n_pages, rows, PAGE_SIZE), F32),
                   jax.ShapeDtypeStruct((rows, PAGE_SIZE), F32),
                   jax.ShapeDtypeStruct((rows, new_pad), F32)],
        compiler_params=_params(("arbitrary",), 16 * _nbytes((rows, PAGE_SIZE), F32)),
        name="sample_bias",
    )(rel_bias.reshape(-1))


def _page_mean_kernel(pps, pt_ref, *refs):
    page_refs, o_ref = refs[:pps], refs[pps]
    ppb = MOBA_BLOCK // PAGE_SIZE
    for b in range(pps // ppb):
        tot = jnp.sum(page_refs[b * ppb][0], axis=0, keepdims=True)
        for u in range(1, ppb):
            tot = tot + jnp.sum(page_refs[b * ppb + u][0], axis=0, keepdims=True)
        o_ref[0, 0, b:b + 1, :] = tot / MOBA_BLOCK


def _page_specs(pps, n_pages):
    def spec(u):
        return pl.BlockSpec((1, PAGE_SIZE, WIDTH), lambda b, c, pt: (pt[b * n_pages + c * pps + u], 0, 0))
    return [spec(u) for u in range(pps)]


def _page_mean(cache_k, page_table):
    db, n_pages = page_table.shape
    ppb = MOBA_BLOCK // PAGE_SIZE
    pps = min(PAGES_PER_STEP, n_pages)
    assert n_pages % pps == 0 and pps % ppb == 0
    nch, bps = n_pages // pps, pps // ppb
    out = pl.pallas_call(
        functools.partial(_page_mean_kernel, pps),
        grid_spec=pltpu.PrefetchScalarGridSpec(
            num_scalar_prefetch=1,
            grid=(db, nch),
            in_specs=_page_specs(pps, n_pages),
            out_specs=pl.BlockSpec((1, 1, bps, WIDTH), lambda b, c, pt: (b, c, 0, 0))),
        out_shape=jax.ShapeDtypeStruct((db, nch, bps, WIDTH), F32),
        compiler_params=_params(("parallel", "arbitrary"), 2 * pps * _nbytes((PAGE_SIZE, WIDTH), F32)),
        name="page_mean",
    )(page_table.reshape(-1), *([cache_k] * pps))
    return out.reshape(db, nch * bps, WIDTH)


def _moba_sample_kernel(pps, n_blocks, t_new, pt_ref, q_ref, mean_ref, knew_ref, vnew_ref,
                        bpast_ref, bpage_ref, bnew_ref, *refs):
    k_refs, v_refs = refs[:pps], refs[pps:2 * pps]
    o_ref = refs[2 * pps]
    qbd_s, sel_s, m_s, l_s, acc_s = refs[2 * pps + 1:]
    del pt_ref
    c = pl.program_id(1)
    nch = pl.num_programs(1)
    ppb = MOBA_BLOCK // PAGE_SIZE
    rows = N_HEADS * t_new
    scale = HEAD_DIM ** -0.5
    ksel = min(MOBA_TOPK, n_blocks)

    @pl.when(c == 0)
    def _():
        q = q_ref[0]
        rep = jnp.concatenate([q] * N_HEADS, axis=0)
        rr = lax.broadcasted_iota(jnp.int32, (rows, WIDTH), 0) // t_new
        cc = lax.broadcasted_iota(jnp.int32, (rows, WIDTH), 1) // HEAD_DIM
        qbd = jnp.where(rr == cc, rep, 0.0)
        qbd_s[...] = qbd
        sc = lax.dot_general(qbd, mean_ref[0], (((1,), (1,)), ((), ())), precision=HIGHEST,
                             preferred_element_type=F32)
        tpos = n_blocks * MOBA_BLOCK + lax.broadcasted_iota(jnp.int32, sc.shape, 0) % t_new
        blk = lax.broadcasted_iota(jnp.int32, sc.shape, 1)
        sel_s[...] = _top_mask(sc, blk < tpos // MOBA_BLOCK, ksel, 1)
        m_s[...] = jnp.full_like(m_s, NEG_INF)
        l_s[...] = jnp.zeros_like(l_s)
        acc_s[...] = jnp.zeros_like(acc_s)

    qb = qbd_s[...].astype(BF16)

    def update(lg, v):
        m = m_s[...]
        m_new = jnp.maximum(m, jnp.max(lg, axis=1, keepdims=True))
        alpha = jnp.exp(m - m_new)
        p = jnp.exp(lg - m_new)
        l_s[...] = alpha * l_s[...] + jnp.sum(p, axis=1, keepdims=True)
        acc_s[...] = alpha * acc_s[...] + _dot(p.astype(BF16), v.astype(BF16))
        m_s[...] = m_new

    sel = sel_s[...]
    blk_lane = lax.broadcasted_iota(jnp.int32, sel.shape, 1)
    for u in range(pps):
        page = c * pps + u
        block = page // ppb
        addm = jnp.max(jnp.where(blk_lane == block, sel, NEG_INF), axis=1, keepdims=True)
        lg = _dot_nt(qb, k_refs[u][0].astype(BF16)) * scale + bpast_ref[page] + addm
        update(lg, v_refs[u][0])

    @pl.when(c == nch - 1)
    def _():
        lg = _dot_nt(qb, k_refs[pps - 1][0].astype(BF16)) * scale + bpage_ref[...]
        update(lg, v_refs[pps - 1][0])
        lg = _dot_nt(qb, knew_ref[0].astype(BF16)) * scale + bnew_ref[...]
        update(lg, vnew_ref[0])
        out = acc_s[...] / l_s[...]
        for h in range(N_HEADS):
            o_ref[0, :, h * HEAD_DIM:(h + 1) * HEAD_DIM] = out[h * t_new:(h + 1) * t_new,
                                                               h * HEAD_DIM:(h + 1) * HEAD_DIM]


def _moba_sample(q, k_new, v_new, cache_k, cache_v, page_table, blk_mean, bias_past, bias_page, bias_new):
    db, t_new, _ = q.shape
    n_pages = page_table.shape[1]
    ppb = MOBA_BLOCK // PAGE_SIZE
    assert ppb == 2 and n_pages % ppb == 0
    n_blocks = n_pages // ppb
    pps = min(PAGES_PER_STEP, n_pages)
    nch = n_pages // pps
    rows = N_HEADS * t_new
    new_pad = bias_new.shape[1]
    seq = lambda *shape: pl.BlockSpec((1,) + shape, lambda b, c, pt: (b,) + (0,) * len(shape))
    const = lambda *shape: pl.BlockSpec(shape, lambda b, c, pt: (0,) * len(shape))
    nbytes = (4 * pps * _nbytes((PAGE_SIZE, WIDTH), F32) + 2 * _nbytes((rows, n_pages * PAGE_SIZE), F32)
              + 16 * _nbytes((rows, WIDTH), F32))
    return pl.pallas_call(
        functools.partial(_moba_sample_kernel, pps, n_blocks, t_new),
        grid_spec=pltpu.PrefetchScalarGridSpec(
            num_scalar_prefetch=1,
            grid=(db, nch),
            in_specs=[seq(t_new, WIDTH), seq(n_blocks, WIDTH), seq(new_pad, WIDTH), seq(new_pad, WIDTH),
                      const(n_pages, rows, PAGE_SIZE), const(rows, PAGE_SIZE), const(rows, new_pad)]
                     + _page_specs(pps, n_pages) + _page_specs(pps, n_pages),
            out_specs=seq(t_new, WIDTH),
            scratch_shapes=[pltpu.VMEM((rows, WIDTH), F32), pltpu.VMEM((rows, n_blocks), F32),
                            pltpu.VMEM((rows, 1), F32), pltpu.VMEM((rows, 1), F32),
                            pltpu.VMEM((rows, WIDTH), F32)]),
        out_shape=jax.ShapeDtypeStruct((db, t_new, WIDTH), F32),
        compiler_params=_params(("parallel", "arbitrary"), nbytes),
        name="moba_sample",
    )(page_table.reshape(-1), q, blk_mean, k_new, v_new, bias_past, bias_page, bias_new,
      *([cache_k] * pps), *([cache_v] * pps))


def _out_ffn_kernel(ff_chunk, x_ref, hm_ref, ha_ref, sgm_ref, sga_ref, gate1_ref, shift2_ref, scale2_ref,
                    gate2_ref, g_ref, wbm_ref, wba_ref, wout_ref, wup_ref, wdn_ref, y_ref):
    bm = _dot(hm_ref[...].astype(BF16), wbm_ref[...])
    ba = _dot(ha_ref[...].astype(BF16), wba_ref[...])
    mix = sgm_ref[...] * bm + sga_ref[...] * ba
    x1 = x_ref[...] + gate1_ref[...] * _dot(mix.astype(BF16), wout_ref[...])
    y = x1 * lax.rsqrt(jnp.mean(x1 * x1, axis=-1, keepdims=True) + NORM_EPS) * g_ref[...]
    h2 = (y * (1.0 + scale2_ref[...]) + shift2_ref[...]).astype(BF16)
    d_ff = wup_ref.shape[1]
    acc = jnp.zeros(x1.shape, F32)
    for c in range(d_ff // ff_chunk):
        u = jnp.maximum(_dot(h2, wup_ref[:, c * ff_chunk:(c + 1) * ff_chunk]), 0.0)
        acc = acc + _dot((u * u).astype(BF16), wdn_ref[c * ff_chunk:(c + 1) * ff_chunk, :])
    y_ref[...] = x1 + gate2_ref[...] * acc


def _out_ffn(x, hm, ha, sgm, sga, gate1, shift2, scale2, gate2, g_norm, wbm, wba, wout, wup, wdn, tm):
    t, d = x.shape
    per_tok = gate1.shape[0] != 1
    tok = lambda n: pl.BlockSpec((tm, n), lambda i: (i, 0))
    mod = tok(d) if per_tok else pl.BlockSpec((1, d), lambda i: (0, 0))
    weights = (wbm, wba, wout, wup, wdn)
    nbytes = (sum(_nbytes(w.shape, w.dtype) for w in weights)
              + 2 * _nbytes((tm, d), F32) * (5 + (4 if per_tok else 0)) + 8 * _nbytes((tm, d), F32))
    return pl.pallas_call(
        functools.partial(_out_ffn_kernel, 1024),
        grid=(t // tm,),
        in_specs=[tok(d), tok(WIDTH), tok(WIDTH), tok(d), tok(d), mod, mod, mod, mod, _const_spec((1, d))]
                 + [_const_spec(w.shape) for w in weights],
        out_specs=tok(d),
        out_shape=jax.ShapeDtypeStruct((t, d), F32),
        compiler_params=_params(("parallel",), nbytes),
        name="out_ffn",
    )(x, hm, ha, sgm, sga, gate1, shift2, scale2, gate2, g_norm.reshape(1, d), *weights)


def _layer(x_prompt, x_sample, cache_k, cache_v, state_c, state_n, state_m, page_table, c_prompt, c_sample,
           rel_bias, w_ada, b_ada, g_norm_mix, w_in, b_igate, b_fgate, g_mhead, g_qnorm, g_knorm,
           w_branch_m, w_branch_a, w_out, g_norm_ffn, w_ff_up, w_ff_down):
    bsz, seq, d = x_prompt.shape
    db, t_new, _ = x_sample.shape
    assert bsz == 1, "the prompt path handles one sequence"
    n_pages = page_table.shape[1]
    past_len = n_pages * PAGE_SIZE
    k_scale = HEAD_DIM ** -0.5

    c_all = jnp.concatenate([c_prompt, c_sample], axis=0)
    pad = (-c_all.shape[0]) % 8
    mod = _ada(jnp.pad(c_all, ((0, pad), (0, 0))), w_ada, b_ada)
    mod_p = jnp.split(mod[0:1], N_COND, axis=-1)
    mod_s = jnp.split(jnp.repeat(mod[1:1 + db], t_new, axis=0), N_COND, axis=-1)

    cols = _split_w_in(w_in)
    gate_col, gate_row = _gate_groups(cols, b_igate, b_fgate)
    gq_row, gk_row = g_qnorm.reshape(1, WIDTH), g_knorm.reshape(1, WIDTH)
    gq_col = g_qnorm.reshape(WIDTH, 1)
    wbm, wba, wout = w_branch_m.astype(BF16), w_branch_a.astype(BF16), w_out.astype(BF16)
    wup, wdn = w_ff_up.astype(BF16), w_ff_down.astype(BF16)

    plan_p = [
        _group(cols["mq"], True), _group(cols["mk"], False, "scale", const=k_scale), _group(cols["mv"], True),
        _group(cols["mo"], True, "sigmoid"), gate_col, gate_row,
        _group(cols["aq"], True, "norm_col", auxv=gq_col),
        _group(cols["ak"], False, "norm_row", auxv=gk_row, outs=(F32, BF16)),
        _group(cols["av"], False), _group(cols["av"], True, outs=(BF16,)),
        _group(cols["gm"], False, "sigmoid"), _group(cols["ga"], False, "sigmoid"),
    ]
    xp = x_prompt.reshape(seq, d)
    (mqT, mk, mvT, soT, gcol, grow, aqT, ak, ak_bf, av, avT_bf, sgm, sga) = _in_proj(
        xp, mod_p[0], mod_p[1], g_norm_mix, plan_p, TOKEN_TILE)
    hm, c_p, n_p, m_p = _mlstm_prompt(mqT, mk, mvT, soT, gcol, grow, g_mhead)
    kmean = _block_mean(ak)
    mask = _prompt_select(kmean, aqT)
    ha = _moba_prompt(aqT, ak_bf, avT_bf, mask, _prompt_bias(rel_bias))
    y_p = _out_ffn(xp, hm, ha, sgm, sga, mod_p[2], mod_p[3], mod_p[4], mod_p[5], g_norm_ffn,
                   wbm, wba, wout, wup, wdn, TOKEN_TILE)

    plan_s = [
        _group(cols["mq"], False), _group(cols["mk"], False, "scale", const=k_scale), _group(cols["mv"], False),
        _group(cols["mo"], False, "sigmoid"), gate_col, gate_row,
        _group(cols["aq"], False, "norm_row", auxv=gq_row),
        _group(cols["ak"], False, "norm_row", auxv=gk_row),
        _group(cols["av"], False),
        _group(cols["gm"], False, "sigmoid"), _group(cols["ga"], False, "sigmoid"),
    ]
    ts = db * t_new
    xs = x_sample.reshape(ts, d)
    (mq_s, mk_s, mv_s, so_s, gcol_s, grow_s, aq_s, ak_s, av_s, sgm_s, sga_s) = _in_proj(
        xs, mod_s[0], mod_s[1], g_norm_mix, plan_s, min(TOKEN_TILE, ts))
    per_seq = lambda a: a.reshape(db, t_new, a.shape[-1])
    mk_seq = per_seq(mk_s)
    grow_seq = jnp.transpose(grow_s, (1, 0, 2)).reshape(GATE_ROWS, db, t_new).transpose(1, 0, 2)
    hm_s, c_s, n_s, m_s = _mlstm_sample(per_seq(mq_s), mk_seq, jnp.swapaxes(mk_seq, 1, 2), per_seq(mv_s),
                                        per_seq(so_s), per_seq(gcol_s), grow_seq,
                                        state_c, state_n, state_m, g_mhead)
    new_pad = -(-t_new // 16) * 16
    pad_new = lambda a: jnp.pad(per_seq(a), ((0, 0), (0, new_pad - t_new), (0, 0)))
    ck = cache_k.reshape(cache_k.shape[0], PAGE_SIZE, WIDTH)
    cv = cache_v.reshape(cache_v.shape[0], PAGE_SIZE, WIDTH)
    blk_mean = _page_mean(ck, page_table)
    b_past, b_page, b_new = _sample_bias(rel_bias, past_len, t_new, new_pad)
    ha_s = _moba_sample(per_seq(aq_s), pad_new(ak_s), pad_new(av_s), ck, cv, page_table, blk_mean,
                        b_past, b_page, b_new)
    y_s = _out_ffn(xs, hm_s.reshape(ts, WIDTH), ha_s.reshape(ts, WIDTH), sgm_s, sga_s,
                   mod_s[2], mod_s[3], mod_s[4], mod_s[5], g_norm_ffn, wbm, wba, wout, wup, wdn,
                   min(TOKEN_TILE, ts))

    heads = lambda a, lead: a.reshape(lead + (N_HEADS, HEAD_DIM))
    return (y_p.reshape(bsz, seq, d), y_s.reshape(db, t_new, d),
            heads(ak, (bsz, seq)), heads(av, (bsz, seq)), c_p[None], n_p[None], m_p[None],
            heads(ak_s, (db, t_new)), heads(av_s, (db, t_new)), c_s, n_s, m_s)


def kernel(x_prompt, x_sample, cache_k, cache_v, state_C, state_n, state_m, page_table, c_prompt, c_sample,
           rel_bias, w_ada, b_ada, g_norm_mix, w_in, b_igate, b_fgate, g_mhead, g_qnorm, g_knorm,
           w_branch_m, w_branch_a, w_out, g_norm_ffn, w_ff_up, w_ff_down):
    depth = w_in.shape[0]
    assert depth == 1, "state outputs are stacked per layer; this kernel implements the single-layer trunk"
    outs = _layer(x_prompt, x_sample, cache_k[0], cache_v[0], state_C[0], state_n[0], state_m[0], page_table,
                  c_prompt, c_sample, rel_bias, w_ada[0], b_ada[0], g_norm_mix[0], w_in[0], b_igate[0],
                  b_fgate[0], g_mhead[0], g_qnorm[0], g_knorm[0], w_branch_m[0], w_branch_a[0], w_out[0],
                  g_norm_ffn[0], w_ff_up[0], w_ff_down[0])
    yp, ys, kp, vp, cp, np_, mp, ks, vs, cs, ns, ms = outs
    stack = lambda a: a[None]
    return (yp, ys, stack(kp), stack(vp), stack(cp), stack(np_), stack(mp),
            stack(ks), stack(vs), stack(cs), stack(ns), stack(ms))
```

```python
import functools
import math

import jax
import jax.numpy as jnp
from jax import lax
from jax.experimental import pallas as pl
from jax.experimental.pallas import tpu as pltpu

F32 = jnp.float32
BF16 = jnp.bfloat16
HIGHEST = lax.Precision.HIGHEST

N_HEADS = 4
HEAD_DIM = 128
WIDTH = N_HEADS * HEAD_DIM
MOBA_BLOCK = 256
MOBA_TOPK = 3
PAGE_SIZE = 128
REL_BUCKETS = 32
REL_MAX_DIST = 4096
N_COND = 6
NORM_EPS = 1e-6
NEG_INF = -1e30
GATE_LANES = 128
GATE_ROWS = 16

V7X_VMEM_BYTES = 64 * 1024 * 1024
VMEM_HEADROOM_BYTES = 8 * 1024 * 1024
TOKEN_TILE = 256
assert TOKEN_TILE == MOBA_BLOCK, "feature-major projection tiles double as MoBA key/query blocks"
PAGES_PER_STEP = 16
KV_GROUP = 4
PAGE_ROWS = PAGE_SIZE * N_HEADS

REL_SAT = 3072


def _vmem_limit(nbytes):
    return int(min(V7X_VMEM_BYTES - VMEM_HEADROOM_BYTES, nbytes + VMEM_HEADROOM_BYTES))


def _nbytes(shape, dtype):
    return math.prod(shape) * jnp.dtype(dtype).itemsize


def _params(sem, nbytes):
    return pltpu.CompilerParams(dimension_semantics=sem, vmem_limit_bytes=_vmem_limit(nbytes))


def _const_spec(shape):
    nd = len(shape)
    return pl.BlockSpec(shape, lambda *_: (0,) * nd, pipeline_mode=pl.Buffered(1))


def _dot(a, b):
    return jnp.dot(a, b, preferred_element_type=F32)


def _dot_nt(a, b):
    return lax.dot_general(a, b, (((1,), (1,)), ((), ())), preferred_element_type=F32)


def _log_sigmoid(x):
    return -(jnp.maximum(-x, 0.0) + jnp.log1p(jnp.exp(-jnp.abs(x))))


def _ada_kernel(c_ref, w_ref, b_ref, o_ref):
    c = c_ref[...]
    s = (c * jax.nn.sigmoid(c)).astype(BF16)
    o_ref[...] = _dot(s, w_ref[...].astype(BF16)) + b_ref[...]


def _ada(c_all, w_ada, b_ada):
    rows, d = c_all.shape
    n_out = w_ada.shape[1]
    tn = d
    nbytes = 2 * (_nbytes((rows, d), F32) + _nbytes((d, tn), F32) + _nbytes((rows, tn), F32))
    return pl.pallas_call(
        _ada_kernel,
        grid=(n_out // tn,),
        in_specs=[pl.BlockSpec((rows, d), lambda j: (0, 0)),
                  pl.BlockSpec((d, tn), lambda j: (0, j)),
                  pl.BlockSpec((1, tn), lambda j: (0, j))],
        out_specs=pl.BlockSpec((rows, tn), lambda j: (0, j)),
        out_shape=jax.ShapeDtypeStruct((rows, n_out), F32),
        compiler_params=_params(("arbitrary",), nbytes),
        name="ada",
    )(c_all, w_ada, b_ada.reshape(1, n_out))


def _head_norm_rows(z, g):
    outs = []
    for h in range(N_HEADS):
        blk = z[:, h * HEAD_DIM:(h + 1) * HEAD_DIM]
        ms = jnp.mean(blk * blk, axis=-1, keepdims=True)
        outs.append(blk * lax.rsqrt(ms + NORM_EPS) * g[:, h * HEAD_DIM:(h + 1) * HEAD_DIM])
    return jnp.concatenate(outs, axis=-1)


def _head_norm_cols(z, g):
    outs = []
    for h in range(N_HEADS):
        blk = z[h * HEAD_DIM:(h + 1) * HEAD_DIM, :]
        ms = jnp.mean(blk * blk, axis=0, keepdims=True)
        outs.append(blk * lax.rsqrt(ms + NORM_EPS) * g[h * HEAD_DIM:(h + 1) * HEAD_DIM, :])
    return jnp.concatenate(outs, axis=0)


def _in_proj_kernel(plan, x_ref, shift_ref, scale_ref, g_ref, *refs):
    n_groups = len(plan)
    w_refs = refs[:n_groups]
    n_aux = sum(1 for p in plan if p["aux"])
    aux_refs = refs[n_groups:n_groups + n_aux]
    out_refs = refs[n_groups + n_aux:]

    x = x_ref[...]
    y = x * lax.rsqrt(jnp.mean(x * x, axis=-1, keepdims=True) + NORM_EPS) * g_ref[...]
    hb = (y * (1.0 + scale_ref[...]) + shift_ref[...]).astype(BF16)

    ai = 0
    oi = 0
    for p, w_ref in zip(plan, w_refs):
        z = _dot_nt(w_ref[...], hb) if p["trans"] else _dot(hb, w_ref[...])
        aux = None
        if p["aux"]:
            aux = aux_refs[ai][...]
            ai += 1
        epi = p["epi"]
        if epi == "scale":
            z = z * p["const"]
        elif epi == "sigmoid":
            z = jax.nn.sigmoid(z)
        elif epi == "norm_row":
            z = _head_norm_rows(z, aux)
        elif epi == "norm_col":
            z = _head_norm_cols(z, aux)
        elif epi == "gate_col":
            pre = z + aux
            lane = lax.broadcasted_iota(jnp.int32, pre.shape, 1)
            z = jnp.where(lane < N_HEADS, pre, _log_sigmoid(pre))
        elif epi == "gate_row":
            pre = z + aux
            row = lax.broadcasted_iota(jnp.int32, pre.shape, 0)
            z = jnp.where(row < N_HEADS, pre, _log_sigmoid(pre))
        for dt in p["outs"]:
            if p["trans"]:
                out_refs[oi][0] = z.astype(dt)
            else:
                out_refs[oi][...] = z.astype(dt)
            oi += 1


def _in_proj(x, shift, scale, g_norm, plan, tm):
    t, d = x.shape
    per_tok = shift.shape[0] != 1
    mod_spec = (pl.BlockSpec((tm, d), lambda i: (i, 0)) if per_tok
                else pl.BlockSpec((1, d), lambda i: (0, 0)))
    in_specs = [pl.BlockSpec((tm, d), lambda i: (i, 0)), mod_spec, mod_spec, _const_spec((1, d))]
    args = [x, shift, scale, g_norm.reshape(1, d)]
    nbytes = 2 * _nbytes((tm, d), F32) * (3 if per_tok else 1)
    for p in plan:
        in_specs.append(_const_spec(p["w"].shape))
        args.append(p["w"])
        nbytes += _nbytes(p["w"].shape, p["w"].dtype)
    for p in plan:
        if p["aux"]:
            in_specs.append(_const_spec(p["auxv"].shape))
            args.append(p["auxv"])
    out_specs, out_shapes = [], []
    for p in plan:
        n = p["w"].shape[0] if p["trans"] else p["w"].shape[1]
        for dt in p["outs"]:
            if p["trans"]:
                out_specs.append(pl.BlockSpec((1, n, tm), lambda i: (i, 0, 0)))
                out_shapes.append(jax.ShapeDtypeStruct((t // tm, n, tm), dt))
            else:
                out_specs.append(pl.BlockSpec((tm, n), lambda i: (i, 0)))
                out_shapes.append(jax.ShapeDtypeStruct((t, n), dt))
            nbytes += 2 * _nbytes((tm, n), dt)
    plan_static = tuple({k: v for k, v in p.items() if k not in ("w", "auxv")} for p in plan)
    return pl.pallas_call(
        functools.partial(_in_proj_kernel, plan_static),
        grid=(t // tm,),
        in_specs=in_specs,
        out_specs=out_specs,
        out_shape=out_shapes,
        compiler_params=_params(("parallel",), nbytes + 4 * _nbytes((tm, 1024), F32)),
        name="in_proj",
    )(*args)


def _group(w, trans, epi="none", const=None, auxv=None, outs=(F32,)):
    w = (w.T if trans else w).astype(BF16)
    return dict(w=w, trans=trans, epi=epi, const=const, aux=auxv is not None, auxv=auxv, outs=tuple(outs))


def _split_w_in(w_in):
    sizes = (WIDTH, WIDTH, WIDTH, WIDTH, N_HEADS, N_HEADS, WIDTH, WIDTH, WIDTH, w_in.shape[0], w_in.shape[0])
    names = ("mq", "mk", "mv", "mo", "mi", "mf", "aq", "ak", "av", "gm", "ga")
    out, off = {}, 0
    for n, s in zip(names, sizes):
        out[n] = w_in[:, off:off + s]
        off += s
    assert off == w_in.shape[1]
    return out


def _gate_groups(cols, b_igate, b_fgate):
    w_gate = jnp.concatenate([cols["mi"], cols["mf"]], axis=1)
    bias = jnp.concatenate([b_igate, b_fgate]).astype(F32)
    pad_to = lambda a, n, axis: jnp.pad(a, [(0, n - a.shape[i]) if i == axis else (0, 0) for i in range(a.ndim)])
    return (_group(pad_to(w_gate, GATE_LANES, 1), False, "gate_col", auxv=pad_to(bias.reshape(1, -1), GATE_LANES, 1)),
            _group(pad_to(w_gate, GATE_ROWS, 1), True, "gate_row", auxv=pad_to(bias.reshape(-1, 1), GATE_ROWS, 0)))


def _mlstm_prompt_kernel(qT_ref, k_ref, vT_ref, oT_ref, gcol_ref, grow_ref, gm_ref,
                         h_ref, c_out_ref, n_out_ref, m_out_ref, c_s, n_s, m_s):
    step = pl.program_id(0)
    L = k_ref.shape[0]

    @pl.when(step == 0)
    def _():
        c_s[...] = jnp.zeros_like(c_s)
        n_s[...] = jnp.zeros_like(n_s)
        m_s[...] = jnp.zeros_like(m_s)

    r = lax.broadcasted_iota(jnp.int32, (L, L), 0)
    c = lax.broadcasted_iota(jnp.int32, (L, L), 1)
    tril = (c <= r).astype(F32)
    triu = (r <= c)
    gcol = gcol_ref[...]
    grow = grow_ref[0]
    bcol_all = jnp.dot(tril, gcol, precision=HIGHEST, preferred_element_type=F32)
    brow_all = jnp.dot(grow, triu.astype(F32), precision=HIGHEST, preferred_element_type=F32)

    for h in range(N_HEADS):
        sl = slice(h * HEAD_DIM, (h + 1) * HEAD_DIM)
        qTb = qT_ref[0, sl, :].astype(BF16)
        k = k_ref[:, sl]
        vTb = vT_ref[0, sl, :].astype(BF16)
        i_col = gcol[:, h:h + 1]
        b_col = bcol_all[:, N_HEADS + h:N_HEADS + h + 1]
        b_row = brow_all[N_HEADS + h:N_HEADS + h + 1, :]
        m_prev = m_s[h][0:1, 0:1]
        ct = c_s[h]
        n_row = n_s[h][0:1, :]

        g_col = i_col - b_col
        dT = jnp.where(triu, g_col + b_row, -jnp.inf)
        m_intra = jnp.max(dT, axis=0, keepdims=True)
        a_row = b_row + m_prev
        m_t = jnp.maximum(a_row, m_intra)
        wT = jnp.exp(dT - m_t)
        w_inter = jnp.exp(a_row - m_t)

        sT = _dot(k.astype(BF16), qTb)
        pT = sT * wT
        den = jnp.sum(pT, axis=0, keepdims=True)
        numT = _dot(vTb, pT.astype(BF16))
        cq = _dot(ct.astype(BF16), qTb)
        nq = _dot(jnp.broadcast_to(n_row, (8, HEAD_DIM)).astype(BF16), qTb)[0:1, :]
        numT = numT + w_inter * cq
        den = den + w_inter * nq
        hT = numT / jnp.maximum(jnp.abs(den), jnp.exp(-m_t))

        ms = jnp.mean(hT * hT, axis=0, keepdims=True)
        yT = hT * lax.rsqrt(ms + NORM_EPS) * gm_ref[sl, :] * oT_ref[0, sl, :]
        h_ref[:, sl] = yT.T

        m_new = m_t[:, L - 1:L]
        b_last = b_row[:, L - 1:L]
        decay = jnp.exp(b_last + m_prev - m_new)
        w_s = jnp.exp(b_last + g_col - m_new)
        kw = k * w_s
        c_new = decay * ct + _dot(vTb, kw.astype(BF16))
        n_new = decay * n_row + jnp.sum(kw, axis=0, keepdims=True)
        c_s[h] = c_new
        n_s[h] = jnp.broadcast_to(n_new, (8, HEAD_DIM))
        m_s[h] = jnp.broadcast_to(m_new, (8, HEAD_DIM))

    @pl.when(step == pl.num_programs(0) - 1)
    def _():
        for h in range(N_HEADS):
            c_out_ref[h] = c_s[h].T
        n_out_ref[...] = n_s[...]
        m_out_ref[...] = m_s[...]


def _mlstm_prompt(qT, k, vT, oT, gcol, grow, g_mhead):
    t = k.shape[0]
    L = qT.shape[2]
    assert t % L == 0 and qT.shape[0] * L == t
    blk_bytes = 3 * _nbytes((WIDTH, L), F32) + 2 * _nbytes((L, WIDTH), F32) + _nbytes((L, GATE_LANES), F32)
    state = jax.ShapeDtypeStruct((N_HEADS, 8, HEAD_DIM), F32)
    outs = pl.pallas_call(
        _mlstm_prompt_kernel,
        grid=(t // L,),
        in_specs=[pl.BlockSpec((1, WIDTH, L), lambda i: (i, 0, 0)),
                  pl.BlockSpec((L, WIDTH), lambda i: (i, 0)),
                  pl.BlockSpec((1, WIDTH, L), lambda i: (i, 0, 0)),
                  pl.BlockSpec((1, WIDTH, L), lambda i: (i, 0, 0)),
                  pl.BlockSpec((L, GATE_LANES), lambda i: (i, 0)),
                  pl.BlockSpec((1, GATE_ROWS, L), lambda i: (i, 0, 0)),
                  _const_spec((WIDTH, 1))],
        out_specs=[pl.BlockSpec((L, WIDTH), lambda i: (i, 0)),
                   pl.BlockSpec((N_HEADS, HEAD_DIM, HEAD_DIM), lambda i: (0, 0, 0)),
                   pl.BlockSpec((N_HEADS, 8, HEAD_DIM), lambda i: (0, 0, 0)),
                   pl.BlockSpec((N_HEADS, 8, HEAD_DIM), lambda i: (0, 0, 0))],
        out_shape=[jax.ShapeDtypeStruct((t, WIDTH), F32),
                   jax.ShapeDtypeStruct((N_HEADS, HEAD_DIM, HEAD_DIM), F32), state, state],
        scratch_shapes=[pltpu.VMEM((N_HEADS, HEAD_DIM, HEAD_DIM), F32),
                        pltpu.VMEM((N_HEADS, 8, HEAD_DIM), F32),
                        pltpu.VMEM((N_HEADS, 8, HEAD_DIM), F32)],
        compiler_params=_params(("arbitrary",), 2 * blk_bytes + 16 * _nbytes((L, L), F32)),
        name="mlstm_prompt",
    )(qT, k, vT, oT, gcol, grow, g_mhead.reshape(WIDTH, 1))
    hm, c_fin, n_fin, m_fin = outs
    return hm, c_fin, n_fin[:, 0, :], m_fin[:, 0, 0]


def _mlstm_sample_kernel(q_ref, k_ref, kT_ref, v_ref, o_ref, gcol_ref, grow_ref, c0_ref, n0_ref, m0_ref,
                         gm_ref, h_ref, c_out_ref, n_out_ref, m_out_ref):
    L = q_ref.shape[1]
    r = lax.broadcasted_iota(jnp.int32, (L, L), 0)
    c = lax.broadcasted_iota(jnp.int32, (L, L), 1)
    causal = c <= r
    gcol = gcol_ref[0]
    grow = grow_ref[0]
    bcol_all = jnp.dot(causal.astype(F32), gcol, precision=HIGHEST, preferred_element_type=F32)
    brow_all = jnp.dot(grow, (r <= c).astype(F32), precision=HIGHEST, preferred_element_type=F32)
    m0 = m0_ref[0]

    for h in range(N_HEADS):
        sl = slice(h * HEAD_DIM, (h + 1) * HEAD_DIM)
        q = q_ref[0][:, sl]
        k = k_ref[0][:, sl]
        kT = kT_ref[0][sl, :]
        v = v_ref[0][:, sl]
        i_col = gcol[:, h:h + 1]
        i_row = grow[h:h + 1, :]
        b_col = bcol_all[:, N_HEADS + h:N_HEADS + h + 1]
        b_row = brow_all[N_HEADS + h:N_HEADS + h + 1, :]
        m_prev = m0[h:h + 1, 0:1]
        c_old = c0_ref[0, h]
        n_old = n0_ref[0][h:h + 1, :]

        dm = jnp.where(causal, b_col - b_row + i_row, -jnp.inf)
        m_intra = jnp.max(dm, axis=1, keepdims=True)
        a_col = b_col + m_prev
        m_t = jnp.maximum(a_col, m_intra)
        w = jnp.exp(dm - m_t)
        w_inter = jnp.exp(a_col - m_t)

        s = _dot_nt(q, k) * w
        num = _dot(s, v) + w_inter * _dot(q.astype(BF16), c_old.astype(BF16))
        qn = jnp.sum(q * n_old, axis=1, keepdims=True)
        den = jnp.sum(s, axis=1, keepdims=True) + w_inter * qn
        hh = num / jnp.maximum(jnp.abs(den), jnp.exp(-m_t))
        ms = jnp.mean(hh * hh, axis=-1, keepdims=True)
        h_ref[0, :, sl] = hh * lax.rsqrt(ms + NORM_EPS) * gm_ref[:, sl] * o_ref[0][:, sl]

        m_new = m_t[L - 1:L, :]
        b_last = b_col[L - 1:L, :]
        decay = jnp.exp(b_last + m_prev - m_new)
        ws_row = jnp.exp(b_last - b_row + i_row - m_new)
        ws_col = jnp.exp(b_last - b_col + i_col - m_new)
        c_out_ref[0, h] = decay * c_old + _dot(kT * ws_row, v)
        n_out_ref[0, h:h + 1, :] = decay * n_old + jnp.sum(k * ws_col, axis=0, keepdims=True)
        m_out_ref[0, h:h + 1, :] = jnp.broadcast_to(m_new, (1, HEAD_DIM))


def _mlstm_sample(q, k, kT, v, o, gcol, grow, c0, n0, m0, g_mhead):
    db, L, _ = q.shape
    seq = lambda *shape: pl.BlockSpec((1,) + shape, lambda b: (b,) + (0,) * len(shape))
    m0_rep = jnp.broadcast_to(m0[:, :, None], (db, N_HEADS, HEAD_DIM))
    nbytes = 2 * 2 * _nbytes((N_HEADS, HEAD_DIM, HEAD_DIM), F32) + 16 * _nbytes((8, WIDTH), F32)
    outs = pl.pallas_call(
        _mlstm_sample_kernel,
        grid=(db,),
        in_specs=[seq(L, WIDTH), seq(L, WIDTH), seq(WIDTH, L), seq(L, WIDTH), seq(L, WIDTH),
                  seq(L, GATE_LANES), seq(GATE_ROWS, L),
                  seq(N_HEADS, HEAD_DIM, HEAD_DIM), seq(N_HEADS, HEAD_DIM), seq(N_HEADS, HEAD_DIM),
                  _const_spec((1, WIDTH))],
        out_specs=[seq(L, WIDTH), seq(N_HEADS, HEAD_DIM, HEAD_DIM), seq(N_HEADS, HEAD_DIM),
                   seq(N_HEADS, HEAD_DIM)],
        out_shape=[jax.ShapeDtypeStruct((db, L, WIDTH), F32),
                   jax.ShapeDtypeStruct((db, N_HEADS, HEAD_DIM, HEAD_DIM), F32),
                   jax.ShapeDtypeStruct((db, N_HEADS, HEAD_DIM), F32),
                   jax.ShapeDtypeStruct((db, N_HEADS, HEAD_DIM), F32)],
        compiler_params=_params(("parallel",), nbytes),
        name="mlstm_sample",
    )(q, k, kT, v, o, gcol, grow, c0, n0, m0_rep, g_mhead.reshape(1, WIDTH))
    hm, c_new, n_new, m_new = outs
    return hm, c_new, n_new, m_new[:, :, 0]


def _rel_bucket(dist):
    max_exact = REL_BUCKETS // 2
    n = jnp.maximum(dist, 0)
    nf = jnp.maximum(n, 1).astype(F32)
    large = max_exact + (jnp.log(nf / max_exact) / math.log(REL_MAX_DIST / max_exact)
                         * (REL_BUCKETS - max_exact)).astype(jnp.int32)
    large = jnp.minimum(large, REL_BUCKETS - 1)
    return jnp.where(n < max_exact, n, large)


def _bias_of_bucket(bucket, rb_ref, h):
    out = jnp.zeros(bucket.shape, F32)
    for b in range(REL_BUCKETS):
        out = jnp.where(bucket == b, rb_ref[b * N_HEADS + h], out)
    return out


def _rel_bias_from_dist(dist, rb_ref, h):
    return _bias_of_bucket(_rel_bucket(dist), rb_ref, h)


def _prompt_bias_kernel(rb_ref, o_ref):
    delta = pl.program_id(0)
    h = pl.program_id(1)
    s = lax.broadcasted_iota(jnp.int32, (MOBA_BLOCK, MOBA_BLOCK), 0)
    t = lax.broadcasted_iota(jnp.int32, (MOBA_BLOCK, MOBA_BLOCK), 1)
    dist = delta * MOBA_BLOCK + t - s
    bias = _rel_bias_from_dist(dist, rb_ref, h) - rb_ref[(REL_BUCKETS - 1) * N_HEADS + h]
    o_ref[0, 0] = jnp.where(dist >= 0, bias, NEG_INF)


def _near_blocks():
    return -(-(REL_SAT + MOBA_BLOCK - 1) // MOBA_BLOCK)


def _prompt_bias(rel_bias):
    nd = _near_blocks()
    return pl.pallas_call(
        _prompt_bias_kernel,
        grid=(nd, N_HEADS),
        in_specs=[pl.BlockSpec(memory_space=pltpu.SMEM)],
        out_specs=pl.BlockSpec((1, 1, MOBA_BLOCK, MOBA_BLOCK), lambda d, h: (d, h, 0, 0)),
        out_shape=jax.ShapeDtypeStruct((nd, N_HEADS, MOBA_BLOCK, MOBA_BLOCK), F32),
        compiler_params=_params(("parallel", "parallel"), 4 * _nbytes((MOBA_BLOCK, MOBA_BLOCK), F32)),
        name="prompt_bias",
    )(rel_bias.reshape(-1))


def _block_mean_kernel(k_ref, o_ref):
    nb = o_ref.shape[0]
    for b in range(nb):
        o_ref[b:b + 1, :] = jnp.mean(k_ref[b * MOBA_BLOCK:(b + 1) * MOBA_BLOCK, :], axis=0, keepdims=True)


def _block_mean(k, nb_step=8):
    t, w = k.shape
    nb = t // MOBA_BLOCK
    nb_step = min(nb_step, nb)
    assert nb % nb_step == 0
    rows = nb_step * MOBA_BLOCK
    return pl.pallas_call(
        _block_mean_kernel,
        grid=(nb // nb_step,),
        in_specs=[pl.BlockSpec((rows, w), lambda i: (i, 0))],
        out_specs=pl.BlockSpec((nb_step, w), lambda i: (i, 0)),
        out_shape=jax.ShapeDtypeStruct((nb, w), F32),
        compiler_params=_params(("parallel",), 2 * _nbytes((rows, w), F32)),
        name="block_mean",
    )(k)


def _top_mask(sc, valid, ksel, axis):
    n = sc.shape[axis]
    iota = lax.broadcasted_iota(jnp.int32, sc.shape, axis)
    cur = jnp.where(valid, sc, NEG_INF)
    sel = jnp.zeros(sc.shape, F32)
    for _ in range(ksel):
        mx = jnp.max(cur, axis=axis, keepdims=True)
        idx = jnp.min(jnp.where(cur == mx, iota, n), axis=axis, keepdims=True)
        hit = iota == idx
        sel = jnp.where(hit, 1.0, sel)
        cur = jnp.where(hit, -jnp.inf, cur)
    return jnp.where((sel > 0.0) & valid, 0.0, NEG_INF)


def _prompt_select_kernel(ksel, kmean_ref, qT_ref, o_ref):
    tq = qT_ref.shape[2]
    nb = kmean_ref.shape[0]
    q0 = pl.program_id(0) * tq
    qpos = q0 + lax.broadcasted_iota(jnp.int32, (nb, tq), 1)
    blk = lax.broadcasted_iota(jnp.int32, (nb, tq), 0)
    valid = blk < qpos // MOBA_BLOCK
    for h in range(N_HEADS):
        sl = slice(h * HEAD_DIM, (h + 1) * HEAD_DIM)
        sc = jnp.dot(kmean_ref[:, sl], qT_ref[0, sl, :], precision=HIGHEST, preferred_element_type=F32)
        o_ref[h] = _top_mask(sc, valid, ksel, 0)


def _prompt_select(kmean, qT):
    nb = kmean.shape[0]
    nt, _, tq = qT.shape
    ksel = min(MOBA_TOPK, nb - 1)
    return pl.pallas_call(
        functools.partial(_prompt_select_kernel, ksel),
        grid=(nt,),
        in_specs=[_const_spec((nb, WIDTH)), pl.BlockSpec((1, WIDTH, tq), lambda i: (i, 0, 0))],
        out_specs=pl.BlockSpec((N_HEADS, nb, tq), lambda i: (0, 0, i)),
        out_shape=jax.ShapeDtypeStruct((N_HEADS, nb, nt * tq), F32),
        compiler_params=_params(("parallel",), 4 * _nbytes((WIDTH, tq), F32)),
        name="prompt_select",
    )(kmean, qT)


def _moba_prompt_kernel(nd, qT_ref, k_ref, vT_ref, mask_ref, bias_ref, o_ref):
    i = pl.program_id(1)
    scale = HEAD_DIM ** -0.5
    qTb = qT_ref[0].astype(BF16)

    def logits(j0, g):
        start = pl.multiple_of(j0 * MOBA_BLOCK, MOBA_BLOCK)
        return _dot(k_ref[pl.ds(start, g * MOBA_BLOCK), :], qTb) * scale

    def group(j0, g, near, carry):
        sT = logits(j0, g)
        parts = []
        for b in range(g):
            s = sT[b * MOBA_BLOCK:(b + 1) * MOBA_BLOCK] + mask_ref[0, pl.ds(j0 + b, 1), :]
            parts.append(s + bias_ref[i - j0 - b, 0] if near else s)
        m, l, acc = carry
        m_new = m
        for s in parts:
            m_new = jnp.maximum(m_new, jnp.max(s, axis=0, keepdims=True))
        alpha = jnp.exp(m - m_new)
        l, acc = alpha * l, alpha * acc
        for b, s in enumerate(parts):
            p = jnp.exp(s - m_new)
            l = l + jnp.sum(p, axis=0, keepdims=True)
            acc = acc + _dot(vT_ref[j0 + b], p.astype(BF16))
        return m_new, l, acc

    def sweep(lo, hi, g, near, carry):
        return lax.fori_loop(0, (hi - lo) // g, lambda t, c: group(lo + t * g, g, near, c), carry)

    sT = logits(i, 1) + bias_ref[0, 0]
    m = jnp.max(sT, axis=0, keepdims=True)
    p = jnp.exp(sT - m)
    carry = (m, jnp.sum(p, axis=0, keepdims=True), _dot(vT_ref[i], p.astype(BF16)))

    first_near = jnp.maximum(i - (nd - 1), 0)
    far_split = first_near // KV_GROUP * KV_GROUP
    near_split = first_near + (i - first_near) // KV_GROUP * KV_GROUP
    carry = sweep(0, far_split, KV_GROUP, False, carry)
    carry = sweep(far_split, first_near, 1, False, carry)
    carry = sweep(first_near, near_split, KV_GROUP, True, carry)
    m, l, acc = sweep(near_split, i, 1, True, carry)
    o_ref[...] = (acc / l).T


def _moba_prompt(qT, k_bf, vT_bf, mask, bias):
    t = k_bf.shape[0]
    nb = t // MOBA_BLOCK
    nd = bias.shape[0]
    assert qT.shape == (nb, WIDTH, MOBA_BLOCK) and vT_bf.shape == qT.shape
    nbytes = (2 * (_nbytes((t, HEAD_DIM), BF16) * 2 + _nbytes((nd, MOBA_BLOCK, MOBA_BLOCK), F32)
                   + _nbytes((nb, MOBA_BLOCK), F32)) + 8 * KV_GROUP * _nbytes((MOBA_BLOCK, MOBA_BLOCK), F32))
    return pl.pallas_call(
        functools.partial(_moba_prompt_kernel, nd),
        grid=(N_HEADS, nb),
        in_specs=[pl.BlockSpec((1, HEAD_DIM, MOBA_BLOCK), lambda h, i: (i, h, 0)),
                  pl.BlockSpec((t, HEAD_DIM), lambda h, i: (0, h)),
                  pl.BlockSpec((nb, HEAD_DIM, MOBA_BLOCK), lambda h, i: (0, h, 0)),
                  pl.BlockSpec((1, nb, MOBA_BLOCK), lambda h, i: (h, 0, i)),
                  pl.BlockSpec((nd, 1, MOBA_BLOCK, MOBA_BLOCK), lambda h, i: (0, h, 0, 0))],
        out_specs=pl.BlockSpec((MOBA_BLOCK, HEAD_DIM), lambda h, i: (i, h)),
        out_shape=jax.ShapeDtypeStruct((t, WIDTH), F32),
        compiler_params=_params(("arbitrary", "arbitrary"), nbytes),
        name="moba_prompt",
    )(qT, k_bf, vT_bf, mask, bias)


def _sample_bias_kernel(past_len, t_new, rb_ref, past_ref, page_ref, new_ref):
    page = pl.program_id(0)

    def table(n_cols, first_kpos, masked):
        shape = (N_HEADS * t_new, n_cols)
        r = lax.broadcasted_iota(jnp.int32, shape, 0)
        c = lax.broadcasted_iota(jnp.int32, shape, 1)
        tpos = past_len + r % t_new
        kpos = first_kpos + c // N_HEADS
        bucket = _rel_bucket(tpos - kpos)
        ok = (kpos >= (tpos // MOBA_BLOCK) * MOBA_BLOCK) & (kpos <= tpos) if masked else None
        out = jnp.full(shape, NEG_INF, F32)
        for h in range(N_HEADS):
            same = (r // t_new == h) & (c % N_HEADS == h)
            out = jnp.where(same if ok is None else same & ok, _bias_of_bucket(bucket, rb_ref, h), out)
        return out

    past_ref[0] = table(PAGE_ROWS, page * PAGE_SIZE, False)
    page_ref[...] = table(PAGE_ROWS, past_len - PAGE_SIZE, True)
    new_ref[...] = table(new_ref.shape[1], past_len, True)


def _sample_bias(rel_bias, past_len, t_new):
    rows = N_HEADS * t_new
    n_pages = past_len // PAGE_SIZE
    new_cols = t_new * N_HEADS
    return pl.pallas_call(
        functools.partial(_sample_bias_kernel, past_len, t_new),
        grid=(n_pages,),
        in_specs=[pl.BlockSpec(memory_space=pltpu.SMEM)],
        out_specs=[pl.BlockSpec((1, rows, PAGE_ROWS), lambda p: (p, 0, 0)),
                   pl.BlockSpec((rows, PAGE_ROWS), lambda p: (0, 0)),
                   pl.BlockSpec((rows, new_cols), lambda p: (0, 0))],
        out_shape=[jax.ShapeDtypeStruct((n_pages, rows, PAGE_ROWS), F32),
                   jax.ShapeDtypeStruct((rows, PAGE_ROWS), F32),
                   jax.ShapeDtypeStruct((rows, new_cols), F32)],
        compiler_params=_params(("arbitrary",), 16 * _nbytes((rows, PAGE_ROWS), F32)),
        name="sample_bias",
    )(rel_bias.reshape(-1))


def _page_specs(pps, n_pages):
    def spec(u):
        return pl.BlockSpec((1, PAGE_ROWS, HEAD_DIM), lambda b, c, pt: (pt[b * n_pages + c * pps + u], 0, 0))
    return [spec(u) for u in range(pps)]


def _softmax_partial(logit_tiles, value_tiles):
    m = jnp.max(logit_tiles[0], axis=1, keepdims=True)
    for lg in logit_tiles[1:]:
        m = jnp.maximum(m, jnp.max(lg, axis=1, keepdims=True))
    l, o = 0.0, 0.0
    for lg, v in zip(logit_tiles, value_tiles):
        p = jnp.exp(lg - m)
        l = l + jnp.sum(p, axis=1, keepdims=True)
        o = o + _dot(p.astype(BF16), v.astype(BF16))
    return m, l, o


def _moba_sample_kernel(pps, n_blocks, t_new, pt_ref, q_ref, knew_ref, vnew_ref,
                        bpast_ref, bpage_ref, bnew_ref, *refs):
    k_refs, v_refs = refs[:pps], refs[pps:2 * pps]
    o_ref = refs[2 * pps]
    sc_s, m_s, l_s, o_s = refs[2 * pps + 1:]
    del pt_ref
    c = pl.program_id(1)
    nch = pl.num_programs(1)
    ppb = MOBA_BLOCK // PAGE_SIZE
    bps = pps // ppb
    rows = N_HEADS * t_new
    fold = 8 // N_HEADS
    scale = HEAD_DIM ** -0.5
    ksel = min(MOBA_TOPK, n_blocks)
    q = q_ref[0]
    qb = q.astype(BF16)
    lanes = lambda a: jnp.broadcast_to(a, (rows, HEAD_DIM))

    for b in range(bps):
        n = c * bps + b
        logit_tiles, value_tiles, ksum = [], [], 0.0
        for u in range(ppb):
            kp = k_refs[b * ppb + u][0]
            logit_tiles.append(_dot_nt(qb, kp.astype(BF16)) * scale + bpast_ref[c * pps + b * ppb + u])
            value_tiles.append(v_refs[b * ppb + u][0])
            ksum = ksum + jnp.sum(kp.reshape(PAGE_ROWS // 8, 8, HEAD_DIM), axis=0)
        kmean = sum(ksum[f * N_HEADS:(f + 1) * N_HEADS] for f in range(fold)) / MOBA_BLOCK
        kmean_rows = jnp.concatenate(
            [jnp.broadcast_to(kmean[h:h + 1], (t_new, HEAD_DIM)) for h in range(N_HEADS)], axis=0)
        sc_s[n] = lanes(jnp.sum(q * kmean_rows, axis=1, keepdims=True))
        m, l, o = _softmax_partial(logit_tiles, value_tiles)
        m_s[n], l_s[n], o_s[n] = lanes(m), lanes(l), o

    @pl.when(c == nch - 1)
    def _():
        own_page = _dot_nt(qb, k_refs[pps - 1][0].astype(BF16)) * scale + bpage_ref[...]
        m_pg, l_pg, o_pg = _softmax_partial([own_page], [v_refs[pps - 1][0]])
        new = _dot_nt(qb, knew_ref[0].astype(BF16)) * scale + bnew_ref[...]
        m_nw, l_nw, o_nw = _softmax_partial([new], [vnew_ref[0]])

        lane = lax.broadcasted_iota(jnp.int32, (rows, HEAD_DIM), 1)
        r = lax.broadcasted_iota(jnp.int32, (rows, HEAD_DIM), 0)
        sc, mm, ll = jnp.zeros((rows, HEAD_DIM), F32), jnp.zeros((rows, HEAD_DIM), F32), jnp.zeros((rows, HEAD_DIM), F32)
        for n in range(n_blocks):
            at = lane == n
            sc, mm, ll = jnp.where(at, sc_s[n], sc), jnp.where(at, m_s[n], mm), jnp.where(at, l_s[n], ll)
        tpos = n_blocks * MOBA_BLOCK + r % t_new
        valid = (lane < tpos // MOBA_BLOCK) & (lane < n_blocks)
        mm = jnp.where(lane < n_blocks, mm + _top_mask(sc, valid, ksel, 1), NEG_INF)
        m_all = jnp.maximum(jnp.max(mm, axis=1, keepdims=True), jnp.maximum(m_pg, m_nw))
        w = jnp.exp(mm - m_all)
        w_pg, w_nw = jnp.exp(m_pg - m_all), jnp.exp(m_nw - m_all)
        den = jnp.sum(w * ll, axis=1, keepdims=True) + w_pg * l_pg + w_nw * l_nw
        num = w_pg * o_pg + w_nw * o_nw
        for n in range(n_blocks):
            num = num + jnp.sum(jnp.where(lane == n, w, 0.0), axis=1, keepdims=True) * o_s[n]
        o_ref[0] = num / den


def _moba_sample(q_rows, k_new, v_new, cache_k, cache_v, page_table, bias_past, bias_page, bias_new):
    db, rows, _ = q_rows.shape
    t_new = rows // N_HEADS
    n_pages = page_table.shape[1]
    ppb = MOBA_BLOCK // PAGE_SIZE
    assert n_pages % ppb == 0 and 8 % N_HEADS == 0 and t_new % 8 == 0
    n_blocks = n_pages // ppb
    assert n_blocks <= HEAD_DIM, "per-block scalars are gathered into one lane tile"
    pps = min(PAGES_PER_STEP, n_pages)
    assert n_pages % pps == 0 and pps % ppb == 0
    seq = lambda *shape: pl.BlockSpec((1,) + shape, lambda b, c, pt: (b,) + (0,) * len(shape))
    slab = pltpu.VMEM((n_blocks, rows, HEAD_DIM), F32)
    nbytes = (4 * pps * _nbytes((PAGE_ROWS, HEAD_DIM), F32) + _nbytes(bias_past.shape, F32)
              + 4 * _nbytes((n_blocks, rows, HEAD_DIM), F32) + 64 * _nbytes((rows, PAGE_ROWS), F32))
    return pl.pallas_call(
        functools.partial(_moba_sample_kernel, pps, n_blocks, t_new),
        grid_spec=pltpu.PrefetchScalarGridSpec(
            num_scalar_prefetch=1,
            grid=(db, n_pages // pps),
            in_specs=[seq(rows, HEAD_DIM), seq(rows, HEAD_DIM), seq(rows, HEAD_DIM),
                      _const_spec(bias_past.shape), _const_spec(bias_page.shape), _const_spec(bias_new.shape)]
                     + _page_specs(pps, n_pages) + _page_specs(pps, n_pages),
            out_specs=seq(rows, HEAD_DIM),
            scratch_shapes=[slab, slab, slab, slab]),
        out_shape=jax.ShapeDtypeStruct((db, rows, HEAD_DIM), F32),
        compiler_params=_params(("parallel", "arbitrary"), nbytes),
        name="moba_sample",
    )(page_table.reshape(-1), q_rows, k_new, v_new, bias_past, bias_page, bias_new,
      *([cache_k] * pps), *([cache_v] * pps))


def _out_ffn_kernel(ff_chunk, x_ref, hm_ref, ha_ref, sgm_ref, sga_ref, gate1_ref, shift2_ref, scale2_ref,
                    gate2_ref, g_ref, wbm_ref, wba_ref, wout_ref, wup_ref, wdn_ref, y_ref):
    bm = _dot(hm_ref[...].astype(BF16), wbm_ref[...])
    ba = _dot(ha_ref[...].astype(BF16), wba_ref[...])
    mix = sgm_ref[...] * bm + sga_ref[...] * ba
    x1 = x_ref[...] + gate1_ref[...] * _dot(mix.astype(BF16), wout_ref[...])
    y = x1 * lax.rsqrt(jnp.mean(x1 * x1, axis=-1, keepdims=True) + NORM_EPS) * g_ref[...]
    h2 = (y * (1.0 + scale2_ref[...]) + shift2_ref[...]).astype(BF16)
    d_ff = wup_ref.shape[1]
    acc = jnp.zeros(x1.shape, F32)
    for c in range(d_ff // ff_chunk):
        u = jnp.maximum(_dot(h2, wup_ref[:, c * ff_chunk:(c + 1) * ff_chunk]), 0.0)
        acc = acc + _dot((u * u).astype(BF16), wdn_ref[c * ff_chunk:(c + 1) * ff_chunk, :])
    y_ref[...] = x1 + gate2_ref[...] * acc


def _out_ffn(x, hm, ha, sgm, sga, gate1, shift2, scale2, gate2, g_norm, wbm, wba, wout, wup, wdn, tm):
    t, d = x.shape
    per_tok = gate1.shape[0] != 1
    tok = lambda n: pl.BlockSpec((tm, n), lambda i: (i, 0))
    mod = tok(d) if per_tok else pl.BlockSpec((1, d), lambda i: (0, 0))
    weights = (wbm, wba, wout, wup, wdn)
    nbytes = (sum(_nbytes(w.shape, w.dtype) for w in weights)
              + 2 * _nbytes((tm, d), F32) * (5 + (4 if per_tok else 0)) + 8 * _nbytes((tm, d), F32))
    return pl.pallas_call(
        functools.partial(_out_ffn_kernel, 1024),
        grid=(t // tm,),
        in_specs=[tok(d), tok(WIDTH), tok(WIDTH), tok(d), tok(d), mod, mod, mod, mod, _const_spec((1, d))]
                 + [_const_spec(w.shape) for w in weights],
        out_specs=tok(d),
        out_shape=jax.ShapeDtypeStruct((t, d), F32),
        compiler_params=_params(("parallel",), nbytes),
        name="out_ffn",
    )(x, hm, ha, sgm, sga, gate1, shift2, scale2, gate2, g_norm.reshape(1, d), *weights)


def _layer(x_prompt, x_sample, cache_k, cache_v, state_c, state_n, state_m, page_table, c_prompt, c_sample,
           rel_bias, w_ada, b_ada, g_norm_mix, w_in, b_igate, b_fgate, g_mhead, g_qnorm, g_knorm,
           w_branch_m, w_branch_a, w_out, g_norm_ffn, w_ff_up, w_ff_down):
    bsz, seq, d = x_prompt.shape
    db, t_new, _ = x_sample.shape
    assert bsz == 1, "the prompt path handles one sequence"
    n_pages = page_table.shape[1]
    past_len = n_pages * PAGE_SIZE
    k_scale = HEAD_DIM ** -0.5

    c_all = jnp.concatenate([c_prompt, c_sample], axis=0)
    pad = (-c_all.shape[0]) % 8
    mod = _ada(jnp.pad(c_all, ((0, pad), (0, 0))), w_ada, b_ada)
    mod_p = jnp.split(mod[0:1], N_COND, axis=-1)
    mod_s = jnp.split(jnp.repeat(mod[1:1 + db], t_new, axis=0), N_COND, axis=-1)

    cols = _split_w_in(w_in)
    gate_col, gate_row = _gate_groups(cols, b_igate, b_fgate)
    gq_row, gk_row = g_qnorm.reshape(1, WIDTH), g_knorm.reshape(1, WIDTH)
    gq_col = g_qnorm.reshape(WIDTH, 1)
    wbm, wba, wout = w_branch_m.astype(BF16), w_branch_a.astype(BF16), w_out.astype(BF16)
    wup, wdn = w_ff_up.astype(BF16), w_ff_down.astype(BF16)

    plan_p = [
        _group(cols["mq"], True), _group(cols["mk"], False, "scale", const=k_scale), _group(cols["mv"], True),
        _group(cols["mo"], True, "sigmoid"), gate_col, gate_row,
        _group(cols["aq"], True, "norm_col", auxv=gq_col),
        _group(cols["ak"], False, "norm_row", auxv=gk_row, outs=(F32, BF16)),
        _group(cols["av"], False), _group(cols["av"], True, outs=(BF16,)),
        _group(cols["gm"], False, "sigmoid"), _group(cols["ga"], False, "sigmoid"),
    ]
    xp = x_prompt.reshape(seq, d)
    (mqT, mk, mvT, soT, gcol, grow, aqT, ak, ak_bf, av, avT_bf, sgm, sga) = _in_proj(
        xp, mod_p[0], mod_p[1], g_norm_mix, plan_p, TOKEN_TILE)
    hm, c_p, n_p, m_p = _mlstm_prompt(mqT, mk, mvT, soT, gcol, grow, g_mhead)
    kmean = _block_mean(ak)
    mask = _prompt_select(kmean, aqT)
    ha = _moba_prompt(aqT, ak_bf, avT_bf, mask, _prompt_bias(rel_bias))
    y_p = _out_ffn(xp, hm, ha, sgm, sga, mod_p[2], mod_p[3], mod_p[4], mod_p[5], g_norm_ffn,
                   wbm, wba, wout, wup, wdn, TOKEN_TILE)

    plan_s = [
        _group(cols["mq"], False), _group(cols["mk"], False, "scale", const=k_scale), _group(cols["mv"], False),
        _group(cols["mo"], False, "sigmoid"), gate_col, gate_row,
        _group(cols["aq"], False, "norm_row", auxv=gq_row),
        _group(cols["ak"], False, "norm_row", auxv=gk_row),
        _group(cols["av"], False),
        _group(cols["gm"], False, "sigmoid"), _group(cols["ga"], False, "sigmoid"),
    ]
    ts = db * t_new
    xs = x_sample.reshape(ts, d)
    (mq_s, mk_s, mv_s, so_s, gcol_s, grow_s, aq_s, ak_s, av_s, sgm_s, sga_s) = _in_proj(
        xs, mod_s[0], mod_s[1], g_norm_mix, plan_s, min(TOKEN_TILE, ts))
    per_seq = lambda a: a.reshape(db, t_new, a.shape[-1])
    mk_seq = per_seq(mk_s)
    grow_seq = jnp.transpose(grow_s, (1, 0, 2)).reshape(GATE_ROWS, db, t_new).transpose(1, 0, 2)
    hm_s, c_s, n_s, m_s = _mlstm_sample(per_seq(mq_s), mk_seq, jnp.swapaxes(mk_seq, 1, 2), per_seq(mv_s),
                                        per_seq(so_s), per_seq(gcol_s), grow_seq,
                                        state_c, state_n, state_m, g_mhead)
    rows = N_HEADS * t_new
    ck = cache_k.reshape(cache_k.shape[0], PAGE_ROWS, HEAD_DIM)
    cv = cache_v.reshape(cache_v.shape[0], PAGE_ROWS, HEAD_DIM)
    head_major = lambda a: a.reshape(db, t_new, N_HEADS, HEAD_DIM).transpose(0, 2, 1, 3).reshape(db, rows, HEAD_DIM)
    b_past, b_page, b_new = _sample_bias(rel_bias, past_len, t_new)
    ha_rows = _moba_sample(head_major(aq_s), ak_s.reshape(db, rows, HEAD_DIM), av_s.reshape(db, rows, HEAD_DIM),
                           ck, cv, page_table, b_past, b_page, b_new)
    ha_s = ha_rows.reshape(db, N_HEADS, t_new, HEAD_DIM).transpose(0, 2, 1, 3)
    y_s = _out_ffn(xs, hm_s.reshape(ts, WIDTH), ha_s.reshape(ts, WIDTH), sgm_s, sga_s,
                   mod_s[2], mod_s[3], mod_s[4], mod_s[5], g_norm_ffn, wbm, wba, wout, wup, wdn,
                   min(TOKEN_TILE, ts))

    heads = lambda a, lead: a.reshape(lead + (N_HEADS, HEAD_DIM))
    return (y_p.reshape(bsz, seq, d), y_s.reshape(db, t_new, d),
            heads(ak, (bsz, seq)), heads(av, (bsz, seq)), c_p[None], n_p[None], m_p[None],
            heads(ak_s, (db, t_new)), heads(av_s, (db, t_new)), c_s, n_s, m_s)


def kernel(x_prompt, x_sample, cache_k, cache_v, state_C, state_n, state_m, page_table, c_prompt, c_sample,
           rel_bias, w_ada, b_ada, g_norm_mix, w_in, b_igate, b_fgate, g_mhead, g_qnorm, g_knorm,
           w_branch_m, w_branch_a, w_out, g_norm_ffn, w_ff_up, w_ff_down):
    depth = w_in.shape[0]
    assert depth == 1, "state outputs are stacked per layer; this kernel implements the single-layer trunk"
    outs = _layer(x_prompt, x_sample, cache_k[0], cache_v[0], state_C[0], state_n[0], state_m[0], page_table,
                  c_prompt, c_sample, rel_bias, w_ada[0], b_ada[0], g_norm_mix[0], w_in[0], b_igate[0],
                  b_fgate[0], g_mhead[0], g_qnorm[0], g_knorm[0], w_branch_m[0], w_branch_a[0], w_out[0],
                  g_norm_ffn[0], w_ff_up[0], w_ff_down[0])
    yp, ys, kp, vp, cp, np_, mp, ks, vs, cs, ns, ms = outs
    stack = lambda a: a[None]
    return (yp, ys, stack(kp), stack(vp), stack(cp), stack(np_), stack(mp),
            stack(ks), stack(vs), stack(cs), stack(ns), stack(ms))
```

```python
import functools
import math

import jax
import jax.numpy as jnp
from jax import lax
from jax.experimental import pallas as pl
from jax.experimental.pallas import tpu as pltpu

F32 = jnp.float32
BF16 = jnp.bfloat16
HIGHEST = lax.Precision.HIGHEST

N_HEADS = 4
HEAD_DIM = 128
WIDTH = N_HEADS * HEAD_DIM
MOBA_BLOCK = 256
MOBA_TOPK = 3
PAGE_SIZE = 128
REL_BUCKETS = 32
REL_MAX_DIST = 4096
N_COND = 6
NORM_EPS = 1e-6
NEG_INF = -1e30
LOG2E = math.log2(math.e)
GATE_LANES = 128
GATE_ROWS = 16

V7X_VMEM_BYTES = 64 * 1024 * 1024
VMEM_HEADROOM_BYTES = 8 * 1024 * 1024
TOKEN_TILE = 256
assert TOKEN_TILE == MOBA_BLOCK, "feature-major projection tiles double as MoBA key/query blocks"
PAGES_PER_STEP = 16
KV_GROUP = 4
PAGE_ROWS = PAGE_SIZE * N_HEADS

REL_SAT = 3072


def _vmem_limit(nbytes):
    return int(min(V7X_VMEM_BYTES - VMEM_HEADROOM_BYTES, nbytes + VMEM_HEADROOM_BYTES))


def _nbytes(shape, dtype):
    return math.prod(shape) * jnp.dtype(dtype).itemsize


def _params(sem, nbytes):
    return pltpu.CompilerParams(dimension_semantics=sem, vmem_limit_bytes=_vmem_limit(nbytes))


def _const_spec(shape):
    nd = len(shape)
    return pl.BlockSpec(shape, lambda *_: (0,) * nd, pipeline_mode=pl.Buffered(1))


def _dot(a, b):
    return jnp.dot(a, b, preferred_element_type=F32)


def _dot_nt(a, b):
    return lax.dot_general(a, b, (((1,), (1,)), ((), ())), preferred_element_type=F32)


def _log_sigmoid(x):
    return -(jnp.maximum(-x, 0.0) + jnp.log1p(jnp.exp(-jnp.abs(x))))


def _ada_kernel(c_ref, w_ref, b_ref, o_ref):
    c = c_ref[...]
    s = (c * jax.nn.sigmoid(c)).astype(BF16)
    o_ref[...] = _dot(s, w_ref[...].astype(BF16)) + b_ref[...]


def _ada(c_all, w_ada, b_ada):
    rows, d = c_all.shape
    n_out = w_ada.shape[1]
    tn = d
    nbytes = 2 * (_nbytes((rows, d), F32) + _nbytes((d, tn), F32) + _nbytes((rows, tn), F32))
    return pl.pallas_call(
        _ada_kernel,
        grid=(n_out // tn,),
        in_specs=[pl.BlockSpec((rows, d), lambda j: (0, 0)),
                  pl.BlockSpec((d, tn), lambda j: (0, j)),
                  pl.BlockSpec((1, tn), lambda j: (0, j))],
        out_specs=pl.BlockSpec((rows, tn), lambda j: (0, j)),
        out_shape=jax.ShapeDtypeStruct((rows, n_out), F32),
        compiler_params=_params(("arbitrary",), nbytes),
        name="ada",
    )(c_all, w_ada, b_ada.reshape(1, n_out))


def _head_norm_rows(z, g):
    outs = []
    for h in range(N_HEADS):
        blk = z[:, h * HEAD_DIM:(h + 1) * HEAD_DIM]
        ms = jnp.mean(blk * blk, axis=-1, keepdims=True)
        outs.append(blk * lax.rsqrt(ms + NORM_EPS) * g[:, h * HEAD_DIM:(h + 1) * HEAD_DIM])
    return jnp.concatenate(outs, axis=-1)


def _head_norm_cols(z, g):
    outs = []
    for h in range(N_HEADS):
        blk = z[h * HEAD_DIM:(h + 1) * HEAD_DIM, :]
        ms = jnp.mean(blk * blk, axis=0, keepdims=True)
        outs.append(blk * lax.rsqrt(ms + NORM_EPS) * g[h * HEAD_DIM:(h + 1) * HEAD_DIM, :])
    return jnp.concatenate(outs, axis=0)


def _in_proj_kernel(plan, x_ref, shift_ref, scale_ref, g_ref, *refs):
    n_groups = len(plan)
    w_refs = refs[:n_groups]
    n_aux = sum(1 for p in plan if p["aux"])
    aux_refs = refs[n_groups:n_groups + n_aux]
    out_refs = refs[n_groups + n_aux:]

    x = x_ref[...]
    y = x * lax.rsqrt(jnp.mean(x * x, axis=-1, keepdims=True) + NORM_EPS) * g_ref[...]
    hb = (y * (1.0 + scale_ref[...]) + shift_ref[...]).astype(BF16)

    ai = 0
    oi = 0
    for p, w_ref in zip(plan, w_refs):
        z = _dot_nt(w_ref[...], hb) if p["trans"] else _dot(hb, w_ref[...])
        aux = None
        if p["aux"]:
            aux = aux_refs[ai][...]
            ai += 1
        epi = p["epi"]
        if epi == "scale":
            z = z * p["const"]
        elif epi == "sigmoid":
            z = jax.nn.sigmoid(z)
        elif epi == "norm_row":
            z = _head_norm_rows(z, aux)
        elif epi == "norm_col":
            z = _head_norm_cols(z, aux)
        elif epi == "gate_col":
            pre = z + aux
            lane = lax.broadcasted_iota(jnp.int32, pre.shape, 1)
            z = jnp.where(lane < N_HEADS, pre, _log_sigmoid(pre))
        elif epi == "gate_row":
            pre = z + aux
            row = lax.broadcasted_iota(jnp.int32, pre.shape, 0)
            z = jnp.where(row < N_HEADS, pre, _log_sigmoid(pre))
        for dt in p["outs"]:
            if p["trans"]:
                out_refs[oi][0] = z.astype(dt)
            else:
                out_refs[oi][...] = z.astype(dt)
            oi += 1


def _in_proj(x, shift, scale, g_norm, plan, tm):
    t, d = x.shape
    per_tok = shift.shape[0] != 1
    mod_spec = (pl.BlockSpec((tm, d), lambda i: (i, 0)) if per_tok
                else pl.BlockSpec((1, d), lambda i: (0, 0)))
    in_specs = [pl.BlockSpec((tm, d), lambda i: (i, 0)), mod_spec, mod_spec, _const_spec((1, d))]
    args = [x, shift, scale, g_norm.reshape(1, d)]
    nbytes = 2 * _nbytes((tm, d), F32) * (3 if per_tok else 1)
    for p in plan:
        in_specs.append(_const_spec(p["w"].shape))
        args.append(p["w"])
        nbytes += _nbytes(p["w"].shape, p["w"].dtype)
    for p in plan:
        if p["aux"]:
            in_specs.append(_const_spec(p["auxv"].shape))
            args.append(p["auxv"])
    out_specs, out_shapes = [], []
    for p in plan:
        n = p["w"].shape[0] if p["trans"] else p["w"].shape[1]
        for dt in p["outs"]:
            if p["trans"]:
                out_specs.append(pl.BlockSpec((1, n, tm), lambda i: (i, 0, 0)))
                out_shapes.append(jax.ShapeDtypeStruct((t // tm, n, tm), dt))
            else:
                out_specs.append(pl.BlockSpec((tm, n), lambda i: (i, 0)))
                out_shapes.append(jax.ShapeDtypeStruct((t, n), dt))
            nbytes += 2 * _nbytes((tm, n), dt)
    plan_static = tuple({k: v for k, v in p.items() if k not in ("w", "auxv")} for p in plan)
    return pl.pallas_call(
        functools.partial(_in_proj_kernel, plan_static),
        grid=(t // tm,),
        in_specs=in_specs,
        out_specs=out_specs,
        out_shape=out_shapes,
        compiler_params=_params(("parallel",), nbytes + 4 * _nbytes((tm, 1024), F32)),
        name="in_proj",
    )(*args)


def _group(w, trans, epi="none", const=None, auxv=None, outs=(F32,)):
    w = (w.T if trans else w).astype(BF16)
    return dict(w=w, trans=trans, epi=epi, const=const, aux=auxv is not None, auxv=auxv, outs=tuple(outs))


def _split_w_in(w_in):
    sizes = (WIDTH, WIDTH, WIDTH, WIDTH, N_HEADS, N_HEADS, WIDTH, WIDTH, WIDTH, w_in.shape[0], w_in.shape[0])
    names = ("mq", "mk", "mv", "mo", "mi", "mf", "aq", "ak", "av", "gm", "ga")
    out, off = {}, 0
    for n, s in zip(names, sizes):
        out[n] = w_in[:, off:off + s]
        off += s
    assert off == w_in.shape[1]
    return out


def _gate_groups(cols, b_igate, b_fgate):
    w_gate = jnp.concatenate([cols["mi"], cols["mf"]], axis=1)
    bias = jnp.concatenate([b_igate, b_fgate]).astype(F32)
    pad_to = lambda a, n, axis: jnp.pad(a, [(0, n - a.shape[i]) if i == axis else (0, 0) for i in range(a.ndim)])
    return (_group(pad_to(w_gate, GATE_LANES, 1), False, "gate_col", auxv=pad_to(bias.reshape(1, -1), GATE_LANES, 1)),
            _group(pad_to(w_gate, GATE_ROWS, 1), True, "gate_row", auxv=pad_to(bias.reshape(-1, 1), GATE_ROWS, 0)))


def _mlstm_prompt_kernel(qT_ref, k_ref, vT_ref, oT_ref, gcol_ref, grow_ref, gm_ref,
                         h_ref, c_out_ref, n_out_ref, m_out_ref, c_s, n_s, m_s):
    step = pl.program_id(0)
    L = k_ref.shape[0]

    @pl.when(step == 0)
    def _():
        c_s[...] = jnp.zeros_like(c_s)
        n_s[...] = jnp.zeros_like(n_s)
        m_s[...] = jnp.zeros_like(m_s)

    r = lax.broadcasted_iota(jnp.int32, (L, L), 0)
    c = lax.broadcasted_iota(jnp.int32, (L, L), 1)
    tril = (c <= r).astype(F32)
    triu = (r <= c)
    gcol = gcol_ref[...]
    grow = grow_ref[0]
    bcol_all = jnp.dot(tril, gcol, precision=HIGHEST, preferred_element_type=F32)
    brow_all = jnp.dot(grow, triu.astype(F32), precision=HIGHEST, preferred_element_type=F32)

    for h in range(N_HEADS):
        sl = slice(h * HEAD_DIM, (h + 1) * HEAD_DIM)
        qTb = qT_ref[0, sl, :].astype(BF16)
        k = k_ref[:, sl]
        vTb = vT_ref[0, sl, :].astype(BF16)
        i_col = gcol[:, h:h + 1]
        b_col = bcol_all[:, N_HEADS + h:N_HEADS + h + 1]
        b_row = brow_all[N_HEADS + h:N_HEADS + h + 1, :]
        m_prev = m_s[h][0:1, 0:1]
        ct = c_s[h]
        n_row = n_s[h][0:1, :]

        g_col = i_col - b_col
        dT = jnp.where(triu, g_col + b_row, -jnp.inf)
        m_intra = jnp.max(dT, axis=0, keepdims=True)
        a_row = b_row + m_prev
        m_t = jnp.maximum(a_row, m_intra)
        wT = jnp.exp(dT - m_t)
        w_inter = jnp.exp(a_row - m_t)

        sT = _dot(k.astype(BF16), qTb)
        pT = sT * wT
        den = jnp.sum(pT, axis=0, keepdims=True)
        numT = _dot(vTb, pT.astype(BF16))
        cq = _dot(ct.astype(BF16), qTb)
        nq = _dot(jnp.broadcast_to(n_row, (8, HEAD_DIM)).astype(BF16), qTb)[0:1, :]
        numT = numT + w_inter * cq
        den = den + w_inter * nq
        hT = numT / jnp.maximum(jnp.abs(den), jnp.exp(-m_t))

        ms = jnp.mean(hT * hT, axis=0, keepdims=True)
        yT = hT * lax.rsqrt(ms + NORM_EPS) * gm_ref[sl, :] * oT_ref[0, sl, :]
        h_ref[:, sl] = yT.T

        m_new = m_t[:, L - 1:L]
        b_last = b_row[:, L - 1:L]
        decay = jnp.exp(b_last + m_prev - m_new)
        w_s = jnp.exp(b_last + g_col - m_new)
        kw = k * w_s
        c_new = decay * ct + _dot(vTb, kw.astype(BF16))
        n_new = decay * n_row + jnp.sum(kw, axis=0, keepdims=True)
        c_s[h] = c_new
        n_s[h] = jnp.broadcast_to(n_new, (8, HEAD_DIM))
        m_s[h] = jnp.broadcast_to(m_new, (8, HEAD_DIM))

    @pl.when(step == pl.num_programs(0) - 1)
    def _():
        for h in range(N_HEADS):
            c_out_ref[h] = c_s[h].T
        n_out_ref[...] = n_s[...]
        m_out_ref[...] = m_s[...]


def _mlstm_prompt(qT, k, vT, oT, gcol, grow, g_mhead):
    t = k.shape[0]
    L = qT.shape[2]
    assert t % L == 0 and qT.shape[0] * L == t
    blk_bytes = 3 * _nbytes((WIDTH, L), F32) + 2 * _nbytes((L, WIDTH), F32) + _nbytes((L, GATE_LANES), F32)
    state = jax.ShapeDtypeStruct((N_HEADS, 8, HEAD_DIM), F32)
    outs = pl.pallas_call(
        _mlstm_prompt_kernel,
        grid=(t // L,),
        in_specs=[pl.BlockSpec((1, WIDTH, L), lambda i: (i, 0, 0)),
                  pl.BlockSpec((L, WIDTH), lambda i: (i, 0)),
                  pl.BlockSpec((1, WIDTH, L), lambda i: (i, 0, 0)),
                  pl.BlockSpec((1, WIDTH, L), lambda i: (i, 0, 0)),
                  pl.BlockSpec((L, GATE_LANES), lambda i: (i, 0)),
                  pl.BlockSpec((1, GATE_ROWS, L), lambda i: (i, 0, 0)),
                  _const_spec((WIDTH, 1))],
        out_specs=[pl.BlockSpec((L, WIDTH), lambda i: (i, 0)),
                   pl.BlockSpec((N_HEADS, HEAD_DIM, HEAD_DIM), lambda i: (0, 0, 0)),
                   pl.BlockSpec((N_HEADS, 8, HEAD_DIM), lambda i: (0, 0, 0)),
                   pl.BlockSpec((N_HEADS, 8, HEAD_DIM), lambda i: (0, 0, 0))],
        out_shape=[jax.ShapeDtypeStruct((t, WIDTH), F32),
                   jax.ShapeDtypeStruct((N_HEADS, HEAD_DIM, HEAD_DIM), F32), state, state],
        scratch_shapes=[pltpu.VMEM((N_HEADS, HEAD_DIM, HEAD_DIM), F32),
                        pltpu.VMEM((N_HEADS, 8, HEAD_DIM), F32),
                        pltpu.VMEM((N_HEADS, 8, HEAD_DIM), F32)],
        compiler_params=_params(("arbitrary",), 2 * blk_bytes + 16 * _nbytes((L, L), F32)),
        name="mlstm_prompt",
    )(qT, k, vT, oT, gcol, grow, g_mhead.reshape(WIDTH, 1))
    hm, c_fin, n_fin, m_fin = outs
    return hm, c_fin, n_fin[:, 0, :], m_fin[:, 0, 0]


def _mlstm_sample_kernel(q_ref, k_ref, kT_ref, v_ref, o_ref, gcol_ref, grow_ref, c0_ref, n0_ref, m0_ref,
                         gm_ref, h_ref, c_out_ref, n_out_ref, m_out_ref):
    L = q_ref.shape[1]
    r = lax.broadcasted_iota(jnp.int32, (L, L), 0)
    c = lax.broadcasted_iota(jnp.int32, (L, L), 1)
    causal = c <= r
    gcol = gcol_ref[0]
    grow = grow_ref[0]
    bcol_all = jnp.dot(causal.astype(F32), gcol, precision=HIGHEST, preferred_element_type=F32)
    brow_all = jnp.dot(grow, (r <= c).astype(F32), precision=HIGHEST, preferred_element_type=F32)
    m0 = m0_ref[0]

    for h in range(N_HEADS):
        sl = slice(h * HEAD_DIM, (h + 1) * HEAD_DIM)
        q = q_ref[0][:, sl]
        k = k_ref[0][:, sl]
        kT = kT_ref[0][sl, :]
        v = v_ref[0][:, sl]
        i_col = gcol[:, h:h + 1]
        i_row = grow[h:h + 1, :]
        b_col = bcol_all[:, N_HEADS + h:N_HEADS + h + 1]
        b_row = brow_all[N_HEADS + h:N_HEADS + h + 1, :]
        m_prev = m0[h:h + 1, 0:1]
        c_old = c0_ref[0, h]
        n_old = n0_ref[0][h:h + 1, :]

        dm = jnp.where(causal, b_col - b_row + i_row, -jnp.inf)
        m_intra = jnp.max(dm, axis=1, keepdims=True)
        a_col = b_col + m_prev
        m_t = jnp.maximum(a_col, m_intra)
        w = jnp.exp(dm - m_t)
        w_inter = jnp.exp(a_col - m_t)

        s = _dot_nt(q, k) * w
        num = _dot(s, v) + w_inter * _dot(q.astype(BF16), c_old.astype(BF16))
        qn = jnp.sum(q * n_old, axis=1, keepdims=True)
        den = jnp.sum(s, axis=1, keepdims=True) + w_inter * qn
        hh = num / jnp.maximum(jnp.abs(den), jnp.exp(-m_t))
        ms = jnp.mean(hh * hh, axis=-1, keepdims=True)
        h_ref[0, :, sl] = hh * lax.rsqrt(ms + NORM_EPS) * gm_ref[:, sl] * o_ref[0][:, sl]

        m_new = m_t[L - 1:L, :]
        b_last = b_col[L - 1:L, :]
        decay = jnp.exp(b_last + m_prev - m_new)
        ws_row = jnp.exp(b_last - b_row + i_row - m_new)
        ws_col = jnp.exp(b_last - b_col + i_col - m_new)
        c_out_ref[0, h] = decay * c_old + _dot(kT * ws_row, v)
        n_out_ref[0, h:h + 1, :] = decay * n_old + jnp.sum(k * ws_col, axis=0, keepdims=True)
        m_out_ref[0, h:h + 1, :] = jnp.broadcast_to(m_new, (1, HEAD_DIM))


def _mlstm_sample(q, k, kT, v, o, gcol, grow, c0, n0, m0, g_mhead):
    db, L, _ = q.shape
    seq = lambda *shape: pl.BlockSpec((1,) + shape, lambda b: (b,) + (0,) * len(shape))
    m0_rep = jnp.broadcast_to(m0[:, :, None], (db, N_HEADS, HEAD_DIM))
    nbytes = 2 * 2 * _nbytes((N_HEADS, HEAD_DIM, HEAD_DIM), F32) + 16 * _nbytes((8, WIDTH), F32)
    outs = pl.pallas_call(
        _mlstm_sample_kernel,
        grid=(db,),
        in_specs=[seq(L, WIDTH), seq(L, WIDTH), seq(WIDTH, L), seq(L, WIDTH), seq(L, WIDTH),
                  seq(L, GATE_LANES), seq(GATE_ROWS, L),
                  seq(N_HEADS, HEAD_DIM, HEAD_DIM), seq(N_HEADS, HEAD_DIM), seq(N_HEADS, HEAD_DIM),
                  _const_spec((1, WIDTH))],
        out_specs=[seq(L, WIDTH), seq(N_HEADS, HEAD_DIM, HEAD_DIM), seq(N_HEADS, HEAD_DIM),
                   seq(N_HEADS, HEAD_DIM)],
        out_shape=[jax.ShapeDtypeStruct((db, L, WIDTH), F32),
                   jax.ShapeDtypeStruct((db, N_HEADS, HEAD_DIM, HEAD_DIM), F32),
                   jax.ShapeDtypeStruct((db, N_HEADS, HEAD_DIM), F32),
                   jax.ShapeDtypeStruct((db, N_HEADS, HEAD_DIM), F32)],
        compiler_params=_params(("parallel",), nbytes),
        name="mlstm_sample",
    )(q, k, kT, v, o, gcol, grow, c0, n0, m0_rep, g_mhead.reshape(1, WIDTH))
    hm, c_new, n_new, m_new = outs
    return hm, c_new, n_new, m_new[:, :, 0]


def _rel_bucket(dist):
    max_exact = REL_BUCKETS // 2
    n = jnp.maximum(dist, 0)
    nf = jnp.maximum(n, 1).astype(F32)
    large = max_exact + (jnp.log(nf / max_exact) / math.log(REL_MAX_DIST / max_exact)
                         * (REL_BUCKETS - max_exact)).astype(jnp.int32)
    large = jnp.minimum(large, REL_BUCKETS - 1)
    return jnp.where(n < max_exact, n, large)


def _bias_of_bucket(bucket, rb_ref, h):
    out = jnp.zeros(bucket.shape, F32)
    for b in range(REL_BUCKETS):
        out = jnp.where(bucket == b, rb_ref[b * N_HEADS + h], out)
    return out


def _rel_bias_from_dist(dist, rb_ref, h):
    return _bias_of_bucket(_rel_bucket(dist), rb_ref, h)


def _prompt_bias_kernel(rb_ref, o_ref):
    delta = pl.program_id(0)
    h = pl.program_id(1)
    s = lax.broadcasted_iota(jnp.int32, (MOBA_BLOCK, MOBA_BLOCK), 0)
    t = lax.broadcasted_iota(jnp.int32, (MOBA_BLOCK, MOBA_BLOCK), 1)
    dist = delta * MOBA_BLOCK + t - s
    bias = _rel_bias_from_dist(dist, rb_ref, h) - rb_ref[(REL_BUCKETS - 1) * N_HEADS + h]
    o_ref[0, 0] = jnp.where(dist >= 0, bias * LOG2E, NEG_INF)


def _near_blocks():
    return -(-(REL_SAT + MOBA_BLOCK - 1) // MOBA_BLOCK)


def _prompt_bias(rel_bias):
    nd = _near_blocks() + 1
    return pl.pallas_call(
        _prompt_bias_kernel,
        grid=(nd, N_HEADS),
        in_specs=[pl.BlockSpec(memory_space=pltpu.SMEM)],
        out_specs=pl.BlockSpec((1, 1, MOBA_BLOCK, MOBA_BLOCK), lambda d, h: (d, h, 0, 0)),
        out_shape=jax.ShapeDtypeStruct((nd, N_HEADS, MOBA_BLOCK, MOBA_BLOCK), F32),
        compiler_params=_params(("parallel", "parallel"), 4 * _nbytes((MOBA_BLOCK, MOBA_BLOCK), F32)),
        name="prompt_bias",
    )(rel_bias.reshape(-1))


def _block_mean_kernel(k_ref, o_ref):
    nb = o_ref.shape[0]
    for b in range(nb):
        o_ref[b:b + 1, :] = jnp.mean(k_ref[b * MOBA_BLOCK:(b + 1) * MOBA_BLOCK, :], axis=0, keepdims=True)


def _block_mean(k, nb_step=8):
    t, w = k.shape
    nb = t // MOBA_BLOCK
    nb_step = min(nb_step, nb)
    assert nb % nb_step == 0
    rows = nb_step * MOBA_BLOCK
    return pl.pallas_call(
        _block_mean_kernel,
        grid=(nb // nb_step,),
        in_specs=[pl.BlockSpec((rows, w), lambda i: (i, 0))],
        out_specs=pl.BlockSpec((nb_step, w), lambda i: (i, 0)),
        out_shape=jax.ShapeDtypeStruct((nb, w), F32),
        compiler_params=_params(("parallel",), 2 * _nbytes((rows, w), F32)),
        name="block_mean",
    )(k)


def _top_mask(sc, valid, ksel, axis):
    n = sc.shape[axis]
    iota = lax.broadcasted_iota(jnp.int32, sc.shape, axis)
    cur = jnp.where(valid, sc, NEG_INF)
    sel = jnp.zeros(sc.shape, F32)
    for _ in range(ksel):
        mx = jnp.max(cur, axis=axis, keepdims=True)
        idx = jnp.min(jnp.where(cur == mx, iota, n), axis=axis, keepdims=True)
        hit = iota == idx
        sel = jnp.where(hit, 1.0, sel)
        cur = jnp.where(hit, -jnp.inf, cur)
    return jnp.where((sel > 0.0) & valid, 0.0, NEG_INF)


def _mask_rows(nb):
    return -(-nb // 16) * 16


def _prompt_select_kernel(ksel, kmean_ref, qT_ref, o_ref):
    tq = qT_ref.shape[2]
    nb = kmean_ref.shape[0]
    nbp = o_ref.shape[2] - HEAD_DIM
    q0 = pl.program_id(0) * tq
    own = (q0 + lax.broadcasted_iota(jnp.int32, (nb, tq), 1)) // MOBA_BLOCK
    blk = lax.broadcasted_iota(jnp.int32, (nb, tq), 0)
    for h in range(N_HEADS):
        sl = slice(h * HEAD_DIM, (h + 1) * HEAD_DIM)
        q = qT_ref[0, sl, :]
        sc = jnp.dot(kmean_ref[:, sl], q, precision=HIGHEST, preferred_element_type=F32)
        mask = jnp.where(blk == own, 0.0, _top_mask(sc, blk < own, ksel, 0))
        o_ref[0, h, 0:HEAD_DIM, :] = (q * (HEAD_DIM ** -0.5 * LOG2E)).astype(BF16)
        o_ref[0, h, HEAD_DIM:HEAD_DIM + nb, :] = mask.astype(BF16)
        if nbp > nb:
            o_ref[0, h, HEAD_DIM + nb:, :] = jnp.zeros((nbp - nb, tq), BF16)


def _prompt_select(kmean, qT):
    nb = kmean.shape[0]
    nt, _, tq = qT.shape
    ksel = min(MOBA_TOPK, nb - 1)
    rows = HEAD_DIM + _mask_rows(nb)
    return pl.pallas_call(
        functools.partial(_prompt_select_kernel, ksel),
        grid=(nt,),
        in_specs=[_const_spec((nb, WIDTH)), pl.BlockSpec((1, WIDTH, tq), lambda i: (i, 0, 0))],
        out_specs=pl.BlockSpec((1, N_HEADS, rows, tq), lambda i: (i, 0, 0, 0)),
        out_shape=jax.ShapeDtypeStruct((nt, N_HEADS, rows, tq), BF16),
        compiler_params=_params(("parallel",), 4 * _nbytes((WIDTH, tq), F32)),
        name="prompt_select",
    )(kmean, qT)


def _moba_prompt_kernel(n_bias, qa_ref, k_ref, vT_ref, bias_ref, o_ref):
    i = pl.program_id(1)
    g_blocks = KV_GROUP
    rows = g_blocks * MOBA_BLOCK
    qa = qa_ref[0, 0]
    n_groups = i // g_blocks + 1
    far_groups = jnp.maximum(i - (n_bias - 2), 0) // g_blocks

    def logits(g):
        start = pl.multiple_of(g * rows, rows)
        return _dot(k_ref[0, pl.ds(start, rows), :], qa)

    def fold(g, sT, near, state):
        m, l, acc = state
        parts = []
        for b in range(g_blocks):
            s = sT[b * MOBA_BLOCK:(b + 1) * MOBA_BLOCK]
            if near:
                s = s + bias_ref[jnp.clip(i - (g * g_blocks + b), 0, n_bias - 1), 0]
            parts.append(s)
        m_new = m
        for s in parts:
            m_new = jnp.maximum(m_new, jnp.max(s, axis=0, keepdims=True))
        alpha = jnp.exp2(m - m_new)
        l, acc = alpha * l, alpha * acc
        for b, s in enumerate(parts):
            p = jnp.exp2(s - m_new)
            l = l + jnp.sum(p, axis=0, keepdims=True)
            acc = acc + _dot(vT_ref[g * g_blocks + b], p.astype(BF16))
        return m_new, l, acc

    def stage(near):
        def body(g, carry):
            m, l, acc, s_cur = carry
            s_next = logits(g + 1)
            return fold(g, s_cur, near, (m, l, acc)) + (s_next,)
        return body

    tq = qa.shape[1]
    carry = (jnp.full((1, tq), NEG_INF, F32), jnp.zeros((1, tq), F32), jnp.zeros((HEAD_DIM, tq), F32), logits(0))
    carry = lax.fori_loop(0, far_groups, stage(False), carry)
    carry = lax.fori_loop(far_groups, n_groups - 1, stage(True), carry)
    m, l, acc = fold(n_groups - 1, carry[3], True, carry[:3])
    o_ref[...] = (acc / l).T


def _moba_prompt(qa, k_aug, vT_bf, bias):
    nb, _, ka, _ = qa.shape
    t = k_aug.shape[1]
    n_bias = bias.shape[0]
    assert t == nb * MOBA_BLOCK and nb % KV_GROUP == 0 and vT_bf.shape == (nb, WIDTH, MOBA_BLOCK)
    nbytes = (2 * (_nbytes((t, -(-ka // 128) * 128), BF16) + _nbytes((t, HEAD_DIM), BF16)
                   + _nbytes((n_bias, MOBA_BLOCK, MOBA_BLOCK), F32))
              + 12 * KV_GROUP * _nbytes((MOBA_BLOCK, MOBA_BLOCK), F32))
    return pl.pallas_call(
        functools.partial(_moba_prompt_kernel, n_bias),
        grid=(N_HEADS, nb),
        in_specs=[pl.BlockSpec((1, 1, ka, MOBA_BLOCK), lambda h, i: (i, h, 0, 0)),
                  pl.BlockSpec((1, t, ka), lambda h, i: (h, 0, 0)),
                  pl.BlockSpec((nb, HEAD_DIM, MOBA_BLOCK), lambda h, i: (0, h, 0)),
                  pl.BlockSpec((n_bias, 1, MOBA_BLOCK, MOBA_BLOCK), lambda h, i: (0, h, 0, 0))],
        out_specs=pl.BlockSpec((MOBA_BLOCK, HEAD_DIM), lambda h, i: (i, h)),
        out_shape=jax.ShapeDtypeStruct((nb * MOBA_BLOCK, WIDTH), F32),
        compiler_params=_params(("arbitrary", "arbitrary"), nbytes),
        name="moba_prompt",
    )(qa, k_aug, vT_bf, bias)


def _sample_bias_kernel(past_len, t_new, rb_ref, past_ref, page_ref, new_ref):
    page = pl.program_id(0)

    def table(n_cols, first_kpos, masked):
        shape = (N_HEADS * t_new, n_cols)
        r = lax.broadcasted_iota(jnp.int32, shape, 0)
        c = lax.broadcasted_iota(jnp.int32, shape, 1)
        tpos = past_len + r % t_new
        kpos = first_kpos + c // N_HEADS
        bucket = _rel_bucket(tpos - kpos)
        ok = (kpos >= (tpos // MOBA_BLOCK) * MOBA_BLOCK) & (kpos <= tpos) if masked else None
        out = jnp.full(shape, NEG_INF, F32)
        for h in range(N_HEADS):
            same = (r // t_new == h) & (c % N_HEADS == h)
            out = jnp.where(same if ok is None else same & ok, _bias_of_bucket(bucket, rb_ref, h) * LOG2E, out)
        return out

    past_ref[0] = table(PAGE_ROWS, page * PAGE_SIZE, False)
    page_ref[...] = table(PAGE_ROWS, past_len - PAGE_SIZE, True)
    new_ref[...] = table(new_ref.shape[1], past_len, True)


def _sample_bias(rel_bias, past_len, t_new):
    rows = N_HEADS * t_new
    n_pages = past_len // PAGE_SIZE
    new_cols = t_new * N_HEADS
    return pl.pallas_call(
        functools.partial(_sample_bias_kernel, past_len, t_new),
        grid=(n_pages,),
        in_specs=[pl.BlockSpec(memory_space=pltpu.SMEM)],
        out_specs=[pl.BlockSpec((1, rows, PAGE_ROWS), lambda p: (p, 0, 0)),
                   pl.BlockSpec((rows, PAGE_ROWS), lambda p: (0, 0)),
                   pl.BlockSpec((rows, new_cols), lambda p: (0, 0))],
        out_shape=[jax.ShapeDtypeStruct((n_pages, rows, PAGE_ROWS), F32),
                   jax.ShapeDtypeStruct((rows, PAGE_ROWS), F32),
                   jax.ShapeDtypeStruct((rows, new_cols), F32)],
        compiler_params=_params(("arbitrary",), 16 * _nbytes((rows, PAGE_ROWS), F32)),
        name="sample_bias",
    )(rel_bias.reshape(-1))


def _page_specs(pps, n_pages):
    def spec(u):
        return pl.BlockSpec((1, PAGE_ROWS, HEAD_DIM), lambda b, c, pt: (pt[b * n_pages + c * pps + u], 0, 0))
    return [spec(u) for u in range(pps)]


def _softmax_partial(logit_tiles, value_tiles):
    m = jnp.max(logit_tiles[0], axis=1, keepdims=True)
    for lg in logit_tiles[1:]:
        m = jnp.maximum(m, jnp.max(lg, axis=1, keepdims=True))
    l, o = 0.0, 0.0
    for lg, v in zip(logit_tiles, value_tiles):
        p = jnp.exp2(lg - m)
        l = l + jnp.sum(p, axis=1, keepdims=True)
        o = o + _dot(p.astype(BF16), v.astype(BF16))
    return m, l, o


def _moba_sample_kernel(pps, n_blocks, t_new, pt_ref, q_ref, knew_ref, vnew_ref,
                        bpast_ref, bpage_ref, bnew_ref, *refs):
    k_refs, v_refs = refs[:pps], refs[pps:2 * pps]
    o_ref = refs[2 * pps]
    sc_s, m_s, l_s, o_s = refs[2 * pps + 1:]
    del pt_ref
    c = pl.program_id(1)
    nch = pl.num_programs(1)
    ppb = MOBA_BLOCK // PAGE_SIZE
    bps = pps // ppb
    rows = N_HEADS * t_new
    fold = 8 // N_HEADS
    ksel = min(MOBA_TOPK, n_blocks)
    q = q_ref[0]
    qb = (q * (HEAD_DIM ** -0.5 * LOG2E)).astype(BF16)
    lanes = lambda a: jnp.broadcast_to(a, (rows, HEAD_DIM))

    pages = [k_refs[u][0] for u in range(pps)]
    tiles = [_dot_nt(qb, pages[u].astype(BF16)) + bpast_ref[c * pps + u] for u in range(pps)]
    for b in range(bps):
        n = c * bps + b
        ksum = sum(jnp.sum(pages[b * ppb + u].reshape(PAGE_ROWS // 8, 8, HEAD_DIM), axis=0)
                   for u in range(ppb))
        kmean = sum(ksum[f * N_HEADS:(f + 1) * N_HEADS] for f in range(fold)) / MOBA_BLOCK
        kmean_rows = jnp.concatenate(
            [jnp.broadcast_to(kmean[h:h + 1], (t_new, HEAD_DIM)) for h in range(N_HEADS)], axis=0)
        sc_s[n] = lanes(jnp.sum(q * kmean_rows, axis=1, keepdims=True))
        m, l, o = _softmax_partial(tiles[b * ppb:(b + 1) * ppb], [v_refs[b * ppb + u][0] for u in range(ppb)])
        m_s[n], l_s[n], o_s[n] = lanes(m), lanes(l), o

    @pl.when(c == nch - 1)
    def _():
        own_page = _dot_nt(qb, pages[pps - 1].astype(BF16)) + bpage_ref[...]
        m_pg, l_pg, o_pg = _softmax_partial([own_page], [v_refs[pps - 1][0]])
        new = _dot_nt(qb, knew_ref[0].astype(BF16)) + bnew_ref[...]
        m_nw, l_nw, o_nw = _softmax_partial([new], [vnew_ref[0]])

        lane = lax.broadcasted_iota(jnp.int32, (rows, HEAD_DIM), 1)
        r = lax.broadcasted_iota(jnp.int32, (rows, HEAD_DIM), 0)
        sc, mm, ll = jnp.zeros((rows, HEAD_DIM), F32), jnp.zeros((rows, HEAD_DIM), F32), jnp.zeros((rows, HEAD_DIM), F32)
        for n in range(n_blocks):
            at = lane == n
            sc, mm, ll = jnp.where(at, sc_s[n], sc), jnp.where(at, m_s[n], mm), jnp.where(at, l_s[n], ll)
        tpos = n_blocks * MOBA_BLOCK + r % t_new
        valid = (lane < tpos // MOBA_BLOCK) & (lane < n_blocks)
        mm = jnp.where(lane < n_blocks, mm + _top_mask(sc, valid, ksel, 1), NEG_INF)
        m_all = jnp.maximum(jnp.max(mm, axis=1, keepdims=True), jnp.maximum(m_pg, m_nw))
        w = jnp.exp2(mm - m_all)
        w_pg, w_nw = jnp.exp2(m_pg - m_all), jnp.exp2(m_nw - m_all)
        den = jnp.sum(w * ll, axis=1, keepdims=True) + w_pg * l_pg + w_nw * l_nw
        num = w_pg * o_pg + w_nw * o_nw
        for n in range(n_blocks):
            num = num + jnp.sum(jnp.where(lane == n, w, 0.0), axis=1, keepdims=True) * o_s[n]
        o_ref[0] = num / den


def _moba_sample(q_rows, k_new, v_new, cache_k, cache_v, page_table, bias_past, bias_page, bias_new):
    db, rows, _ = q_rows.shape
    t_new = rows // N_HEADS
    n_pages = page_table.shape[1]
    ppb = MOBA_BLOCK // PAGE_SIZE
    assert n_pages % ppb == 0 and 8 % N_HEADS == 0 and t_new % 8 == 0
    n_blocks = n_pages // ppb
    assert n_blocks <= HEAD_DIM, "per-block scalars are gathered into one lane tile"
    pps = min(PAGES_PER_STEP, n_pages)
    assert n_pages % pps == 0 and pps % ppb == 0
    seq = lambda *shape: pl.BlockSpec((1,) + shape, lambda b, c, pt: (b,) + (0,) * len(shape))
    slab = pltpu.VMEM((n_blocks, rows, HEAD_DIM), F32)
    nbytes = (4 * pps * _nbytes((PAGE_ROWS, HEAD_DIM), F32) + _nbytes(bias_past.shape, F32)
              + 4 * _nbytes((n_blocks, rows, HEAD_DIM), F32) + 64 * _nbytes((rows, PAGE_ROWS), F32))
    return pl.pallas_call(
        functools.partial(_moba_sample_kernel, pps, n_blocks, t_new),
        grid_spec=pltpu.PrefetchScalarGridSpec(
            num_scalar_prefetch=1,
            grid=(db, n_pages // pps),
            in_specs=[seq(rows, HEAD_DIM), seq(rows, HEAD_DIM), seq(rows, HEAD_DIM),
                      _const_spec(bias_past.shape), _const_spec(bias_page.shape), _const_spec(bias_new.shape)]
                     + _page_specs(pps, n_pages) + _page_specs(pps, n_pages),
            out_specs=seq(rows, HEAD_DIM),
            scratch_shapes=[slab, slab, slab, slab]),
        out_shape=jax.ShapeDtypeStruct((db, rows, HEAD_DIM), F32),
        compiler_params=_params(("parallel", "arbitrary"), nbytes),
        name="moba_sample",
    )(page_table.reshape(-1), q_rows, k_new, v_new, bias_past, bias_page, bias_new,
      *([cache_k] * pps), *([cache_v] * pps))


def _out_ffn_kernel(ff_chunk, x_ref, hm_ref, ha_ref, sgm_ref, sga_ref, gate1_ref, shift2_ref, scale2_ref,
                    gate2_ref, g_ref, wbm_ref, wba_ref, wout_ref, wup_ref, wdn_ref, y_ref):
    bm = _dot(hm_ref[...].astype(BF16), wbm_ref[...])
    ba = _dot(ha_ref[...].astype(BF16), wba_ref[...])
    mix = sgm_ref[...] * bm + sga_ref[...] * ba
    x1 = x_ref[...] + gate1_ref[...] * _dot(mix.astype(BF16), wout_ref[...])
    y = x1 * lax.rsqrt(jnp.mean(x1 * x1, axis=-1, keepdims=True) + NORM_EPS) * g_ref[...]
    h2 = (y * (1.0 + scale2_ref[...]) + shift2_ref[...]).astype(BF16)
    d_ff = wup_ref.shape[1]
    acc = jnp.zeros(x1.shape, F32)
    for c in range(d_ff // ff_chunk):
        u = jnp.maximum(_dot(h2, wup_ref[:, c * ff_chunk:(c + 1) * ff_chunk]), 0.0)
        acc = acc + _dot((u * u).astype(BF16), wdn_ref[c * ff_chunk:(c + 1) * ff_chunk, :])
    y_ref[...] = x1 + gate2_ref[...] * acc


def _out_ffn(x, hm, ha, sgm, sga, gate1, shift2, scale2, gate2, g_norm, wbm, wba, wout, wup, wdn, tm):
    t, d = x.shape
    per_tok = gate1.shape[0] != 1
    tok = lambda n: pl.BlockSpec((tm, n), lambda i: (i, 0))
    mod = tok(d) if per_tok else pl.BlockSpec((1, d), lambda i: (0, 0))
    weights = (wbm, wba, wout, wup, wdn)
    nbytes = (sum(_nbytes(w.shape, w.dtype) for w in weights)
              + 2 * _nbytes((tm, d), F32) * (5 + (4 if per_tok else 0)) + 8 * _nbytes((tm, d), F32))
    return pl.pallas_call(
        functools.partial(_out_ffn_kernel, 1024),
        grid=(t // tm,),
        in_specs=[tok(d), tok(WIDTH), tok(WIDTH), tok(d), tok(d), mod, mod, mod, mod, _const_spec((1, d))]
                 + [_const_spec(w.shape) for w in weights],
        out_specs=tok(d),
        out_shape=jax.ShapeDtypeStruct((t, d), F32),
        compiler_params=_params(("parallel",), nbytes),
        name="out_ffn",
    )(x, hm, ha, sgm, sga, gate1, shift2, scale2, gate2, g_norm.reshape(1, d), *weights)


def _layer(x_prompt, x_sample, cache_k, cache_v, state_c, state_n, state_m, page_table, c_prompt, c_sample,
           rel_bias, w_ada, b_ada, g_norm_mix, w_in, b_igate, b_fgate, g_mhead, g_qnorm, g_knorm,
           w_branch_m, w_branch_a, w_out, g_norm_ffn, w_ff_up, w_ff_down):
    bsz, seq, d = x_prompt.shape
    db, t_new, _ = x_sample.shape
    assert bsz == 1, "the prompt path handles one sequence"
    n_pages = page_table.shape[1]
    past_len = n_pages * PAGE_SIZE
    k_scale = HEAD_DIM ** -0.5

    c_all = jnp.concatenate([c_prompt, c_sample], axis=0)
    pad = (-c_all.shape[0]) % 8
    mod = _ada(jnp.pad(c_all, ((0, pad), (0, 0))), w_ada, b_ada)
    mod_p = jnp.split(mod[0:1], N_COND, axis=-1)
    mod_s = jnp.split(jnp.repeat(mod[1:1 + db], t_new, axis=0), N_COND, axis=-1)

    cols = _split_w_in(w_in)
    gate_col, gate_row = _gate_groups(cols, b_igate, b_fgate)
    gq_row, gk_row = g_qnorm.reshape(1, WIDTH), g_knorm.reshape(1, WIDTH)
    gq_col = g_qnorm.reshape(WIDTH, 1)
    wbm, wba, wout = w_branch_m.astype(BF16), w_branch_a.astype(BF16), w_out.astype(BF16)
    wup, wdn = w_ff_up.astype(BF16), w_ff_down.astype(BF16)

    plan_p = [
        _group(cols["mq"], True), _group(cols["mk"], False, "scale", const=k_scale), _group(cols["mv"], True),
        _group(cols["mo"], True, "sigmoid"), gate_col, gate_row,
        _group(cols["aq"], True, "norm_col", auxv=gq_col),
        _group(cols["ak"], False, "norm_row", auxv=gk_row, outs=(F32, BF16)),
        _group(cols["av"], False), _group(cols["av"], True, outs=(BF16,)),
        _group(cols["gm"], False, "sigmoid"), _group(cols["ga"], False, "sigmoid"),
    ]
    xp = x_prompt.reshape(seq, d)
    (mqT, mk, mvT, soT, gcol, grow, aqT, ak, ak_bf, av, avT_bf, sgm, sga) = _in_proj(
        xp, mod_p[0], mod_p[1], g_norm_mix, plan_p, TOKEN_TILE)
    hm, c_p, n_p, m_p = _mlstm_prompt(mqT, mk, mvT, soT, gcol, grow, g_mhead)
    kmean = _block_mean(ak)
    qa = _prompt_select(kmean, aqT)
    nb = seq // MOBA_BLOCK
    block_id = jnp.arange(seq, dtype=jnp.int32)[:, None] // MOBA_BLOCK
    one_hot = (block_id == jnp.arange(_mask_rows(nb), dtype=jnp.int32)[None, :]).astype(BF16)
    k_aug = jnp.concatenate([ak_bf.reshape(seq, N_HEADS, HEAD_DIM).transpose(1, 0, 2),
                             jnp.broadcast_to(one_hot, (N_HEADS,) + one_hot.shape)], axis=-1)
    ha = _moba_prompt(qa, k_aug, avT_bf, _prompt_bias(rel_bias))
    y_p = _out_ffn(xp, hm, ha, sgm, sga, mod_p[2], mod_p[3], mod_p[4], mod_p[5], g_norm_ffn,
                   wbm, wba, wout, wup, wdn, TOKEN_TILE)

    plan_s = [
        _group(cols["mq"], False), _group(cols["mk"], False, "scale", const=k_scale), _group(cols["mv"], False),
        _group(cols["mo"], False, "sigmoid"), gate_col, gate_row,
        _group(cols["aq"], False, "norm_row", auxv=gq_row),
        _group(cols["ak"], False, "norm_row", auxv=gk_row),
        _group(cols["av"], False),
        _group(cols["gm"], False, "sigmoid"), _group(cols["ga"], False, "sigmoid"),
    ]
    ts = db * t_new
    xs = x_sample.reshape(ts, d)
    (mq_s, mk_s, mv_s, so_s, gcol_s, grow_s, aq_s, ak_s, av_s, sgm_s, sga_s) = _in_proj(
        xs, mod_s[0], mod_s[1], g_norm_mix, plan_s, min(TOKEN_TILE, ts))
    per_seq = lambda a: a.reshape(db, t_new, a.shape[-1])
    mk_seq = per_seq(mk_s)
    grow_seq = jnp.transpose(grow_s, (1, 0, 2)).reshape(GATE_ROWS, db, t_new).transpose(1, 0, 2)
    hm_s, c_s, n_s, m_s = _mlstm_sample(per_seq(mq_s), mk_seq, jnp.swapaxes(mk_seq, 1, 2), per_seq(mv_s),
                                        per_seq(so_s), per_seq(gcol_s), grow_seq,
                                        state_c, state_n, state_m, g_mhead)
    rows = N_HEADS * t_new
    ck = cache_k.reshape(cache_k.shape[0], PAGE_ROWS, HEAD_DIM)
    cv = cache_v.reshape(cache_v.shape[0], PAGE_ROWS, HEAD_DIM)
    head_major = lambda a: a.reshape(db, t_new, N_HEADS, HEAD_DIM).transpose(0, 2, 1, 3).reshape(db, rows, HEAD_DIM)
    b_past, b_page, b_new = _sample_bias(rel_bias, past_len, t_new)
    ha_rows = _moba_sample(head_major(aq_s), ak_s.reshape(db, rows, HEAD_DIM), av_s.reshape(db, rows, HEAD_DIM),
                           ck, cv, page_table, b_past, b_page, b_new)
    ha_s = ha_rows.reshape(db, N_HEADS, t_new, HEAD_DIM).transpose(0, 2, 1, 3)
    y_s = _out_ffn(xs, hm_s.reshape(ts, WIDTH), ha_s.reshape(ts, WIDTH), sgm_s, sga_s,
                   mod_s[2], mod_s[3], mod_s[4], mod_s[5], g_norm_ffn, wbm, wba, wout, wup, wdn,
                   min(TOKEN_TILE, ts))

    heads = lambda a, lead: a.reshape(lead + (N_HEADS, HEAD_DIM))
    return (y_p.reshape(bsz, seq, d), y_s.reshape(db, t_new, d),
            heads(ak, (bsz, seq)), heads(av, (bsz, seq)), c_p[None], n_p[None], m_p[None],
            heads(ak_s, (db, t_new)), heads(av_s, (db, t_new)), c_s, n_s, m_s)


def kernel(x_prompt, x_sample, cache_k, cache_v, state_C, state_n, state_m, page_table, c_prompt, c_sample,
           rel_bias, w_ada, b_ada, g_norm_mix, w_in, b_igate, b_fgate, g_mhead, g_qnorm, g_knorm,
           w_branch_m, w_branch_a, w_out, g_norm_ffn, w_ff_up, w_ff_down):
    depth = w_in.shape[0]
    assert depth == 1, "state outputs are stacked per layer; this kernel implements the single-layer trunk"
    outs = _layer(x_prompt, x_sample, cache_k[0], cache_v[0], state_C[0], state_n[0], state_m[0], page_table,
                  c_prompt, c_sample, rel_bias, w_ada[0], b_ada[0], g_norm_mix[0], w_in[0], b_igate[0],
                  b_fgate[0], g_mhead[0], g_qnorm[0], g_knorm[0], w_branch_m[0], w_branch_a[0], w_out[0],
                  g_norm_ffn[0], w_ff_up[0], w_ff_down[0])
    yp, ys, kp, vp, cp, np_, mp, ks, vs, cs, ns, ms = outs
    stack = lambda a: a[None]
    return (yp, ys, stack(kp), stack(vp), stack(cp), stack(np_), stack(mp),
            stack(ks), stack(vs), stack(cs), stack(ns), stack(ms))
```

```python
import functools
import math

import jax
import jax.numpy as jnp
from jax import lax
from jax.experimental import pallas as pl
from jax.experimental.pallas import tpu as pltpu

F32 = jnp.float32
BF16 = jnp.bfloat16
HIGHEST = lax.Precision.HIGHEST

N_HEADS = 4
HEAD_DIM = 128
WIDTH = N_HEADS * HEAD_DIM
MOBA_BLOCK = 256
MOBA_TOPK = 3
PAGE_SIZE = 128
REL_BUCKETS = 32
REL_MAX_DIST = 4096
N_COND = 6
NORM_EPS = 1e-6
NEG_INF = -1e30
LOG2E = math.log2(math.e)
GATE_LANES = 128
GATE_ROWS = 16

V7X_VMEM_BYTES = 64 * 1024 * 1024
VMEM_HEADROOM_BYTES = 8 * 1024 * 1024
TOKEN_TILE = 256
assert TOKEN_TILE == MOBA_BLOCK, "feature-major projection tiles double as MoBA key/query blocks"
PAGES_PER_STEP = 32
KV_GROUP = 4
SEQS_PER_STEP = 2
PAGE_ROWS = PAGE_SIZE * N_HEADS

REL_SAT = 3072


def _vmem_limit(nbytes):
    return int(min(V7X_VMEM_BYTES - VMEM_HEADROOM_BYTES, nbytes + VMEM_HEADROOM_BYTES))


def _nbytes(shape, dtype):
    return math.prod(shape) * jnp.dtype(dtype).itemsize


def _params(sem, nbytes):
    return pltpu.CompilerParams(dimension_semantics=sem, vmem_limit_bytes=_vmem_limit(nbytes))


def _const_spec(shape):
    nd = len(shape)
    return pl.BlockSpec(shape, lambda *_: (0,) * nd, pipeline_mode=pl.Buffered(1))


def _dot(a, b):
    return jnp.dot(a, b, preferred_element_type=F32)


def _dot_nt(a, b):
    return lax.dot_general(a, b, (((1,), (1,)), ((), ())), preferred_element_type=F32)


def _log_sigmoid(x):
    return -(jnp.maximum(-x, 0.0) + jnp.log1p(jnp.exp(-jnp.abs(x))))


def _ada_kernel(c_ref, w_ref, b_ref, o_ref):
    c = c_ref[...]
    s = (c * jax.nn.sigmoid(c)).astype(BF16)
    o_ref[...] = _dot(s, w_ref[...].astype(BF16)) + b_ref[...]


def _ada(c_all, w_ada, b_ada):
    rows, d = c_all.shape
    n_out = w_ada.shape[1]
    tn = d
    nbytes = 2 * (_nbytes((rows, d), F32) + _nbytes((d, tn), F32) + _nbytes((rows, tn), F32))
    return pl.pallas_call(
        _ada_kernel,
        grid=(n_out // tn,),
        in_specs=[pl.BlockSpec((rows, d), lambda j: (0, 0)),
                  pl.BlockSpec((d, tn), lambda j: (0, j)),
                  pl.BlockSpec((1, tn), lambda j: (0, j))],
        out_specs=pl.BlockSpec((rows, tn), lambda j: (0, j)),
        out_shape=jax.ShapeDtypeStruct((rows, n_out), F32),
        compiler_params=_params(("arbitrary",), nbytes),
        name="ada",
    )(c_all, w_ada, b_ada.reshape(1, n_out))


def _head_norm_rows(z, g):
    outs = []
    for h in range(N_HEADS):
        blk = z[:, h * HEAD_DIM:(h + 1) * HEAD_DIM]
        ms = jnp.mean(blk * blk, axis=-1, keepdims=True)
        outs.append(blk * lax.rsqrt(ms + NORM_EPS) * g[:, h * HEAD_DIM:(h + 1) * HEAD_DIM])
    return jnp.concatenate(outs, axis=-1)


def _head_norm_cols(z, g):
    outs = []
    for h in range(N_HEADS):
        blk = z[h * HEAD_DIM:(h + 1) * HEAD_DIM, :]
        ms = jnp.mean(blk * blk, axis=0, keepdims=True)
        outs.append(blk * lax.rsqrt(ms + NORM_EPS) * g[h * HEAD_DIM:(h + 1) * HEAD_DIM, :])
    return jnp.concatenate(outs, axis=0)


def _in_proj_kernel(plan, x_ref, shift_ref, scale_ref, g_ref, *refs):
    n_groups = len(plan)
    w_refs = refs[:n_groups]
    n_aux = sum(1 for p in plan if p["aux"])
    aux_refs = refs[n_groups:n_groups + n_aux]
    out_refs = refs[n_groups + n_aux:]

    x = x_ref[...]
    y = x * lax.rsqrt(jnp.mean(x * x, axis=-1, keepdims=True) + NORM_EPS) * g_ref[...]
    hb = (y * (1.0 + scale_ref[...]) + shift_ref[...]).astype(BF16)

    ai = 0
    oi = 0
    for p, w_ref in zip(plan, w_refs):
        z = _dot_nt(w_ref[...], hb) if p["trans"] else _dot(hb, w_ref[...])
        aux = None
        if p["aux"]:
            aux = aux_refs[ai][...]
            ai += 1
        epi = p["epi"]
        if epi == "scale":
            z = z * p["const"]
        elif epi == "sigmoid":
            z = jax.nn.sigmoid(z)
        elif epi == "norm_row":
            z = _head_norm_rows(z, aux)
        elif epi == "norm_col":
            z = _head_norm_cols(z, aux)
        elif epi == "gate_col":
            pre = z + aux
            lane = lax.broadcasted_iota(jnp.int32, pre.shape, 1)
            z = jnp.where(lane < N_HEADS, pre, _log_sigmoid(pre))
        elif epi == "gate_row":
            pre = z + aux
            row = lax.broadcasted_iota(jnp.int32, pre.shape, 0)
            z = jnp.where(row < N_HEADS, pre, _log_sigmoid(pre))
        for kind in p["outs"]:
            if kind == "heads":
                for h in range(N_HEADS):
                    out_refs[oi][:, h, :] = z[:, h * HEAD_DIM:(h + 1) * HEAD_DIM]
            elif kind == "mean":
                out_refs[oi][0] = jnp.mean(z, axis=0, keepdims=True)
            elif p["trans"]:
                out_refs[oi][0] = z.astype(kind)
            else:
                out_refs[oi][...] = z.astype(kind)
            oi += 1


def _in_proj(x, shift, scale, g_norm, plan, tm):
    t, d = x.shape
    per_tok = shift.shape[0] != 1
    mod_spec = (pl.BlockSpec((tm, d), lambda i: (i, 0)) if per_tok
                else pl.BlockSpec((1, d), lambda i: (0, 0)))
    in_specs = [pl.BlockSpec((tm, d), lambda i: (i, 0)), mod_spec, mod_spec, _const_spec((1, d))]
    args = [x, shift, scale, g_norm.reshape(1, d)]
    nbytes = 2 * _nbytes((tm, d), F32) * (3 if per_tok else 1)
    for p in plan:
        in_specs.append(_const_spec(p["w"].shape))
        args.append(p["w"])
        nbytes += _nbytes(p["w"].shape, p["w"].dtype)
    for p in plan:
        if p["aux"]:
            in_specs.append(_const_spec(p["auxv"].shape))
            args.append(p["auxv"])
    out_specs, out_shapes = [], []
    for p in plan:
        n = p["w"].shape[0] if p["trans"] else p["w"].shape[1]
        for kind in p["outs"]:
            if kind == "heads":
                assert not p["trans"] and n == WIDTH
                out_specs.append(pl.BlockSpec((tm, N_HEADS, HEAD_DIM), lambda i: (i, 0, 0)))
                out_shapes.append(jax.ShapeDtypeStruct((t, N_HEADS, HEAD_DIM), F32))
            elif kind == "mean":
                assert not p["trans"] and tm == MOBA_BLOCK
                out_specs.append(pl.BlockSpec((1, 1, n), lambda i: (i, 0, 0)))
                out_shapes.append(jax.ShapeDtypeStruct((t // tm, 1, n), F32))
            elif p["trans"]:
                out_specs.append(pl.BlockSpec((1, n, tm), lambda i: (i, 0, 0)))
                out_shapes.append(jax.ShapeDtypeStruct((t // tm, n, tm), kind))
            else:
                out_specs.append(pl.BlockSpec((tm, n), lambda i: (i, 0)))
                out_shapes.append(jax.ShapeDtypeStruct((t, n), kind))
            nbytes += 2 * _nbytes((tm, n), F32)
    plan_static = tuple({k: v for k, v in p.items() if k not in ("w", "auxv")} for p in plan)
    return pl.pallas_call(
        functools.partial(_in_proj_kernel, plan_static),
        grid=(t // tm,),
        in_specs=in_specs,
        out_specs=out_specs,
        out_shape=out_shapes,
        compiler_params=_params(("parallel",), nbytes + 4 * _nbytes((tm, 1024), F32)),
        name="in_proj",
    )(*args)


def _group(w, trans, epi="none", const=None, auxv=None, outs=(F32,)):
    w = (w.T if trans else w).astype(BF16)
    return dict(w=w, trans=trans, epi=epi, const=const, aux=auxv is not None, auxv=auxv, outs=tuple(outs))


def _split_w_in(w_in):
    sizes = (WIDTH, WIDTH, WIDTH, WIDTH, N_HEADS, N_HEADS, WIDTH, WIDTH, WIDTH, w_in.shape[0], w_in.shape[0])
    names = ("mq", "mk", "mv", "mo", "mi", "mf", "aq", "ak", "av", "gm", "ga")
    out, off = {}, 0
    for n, s in zip(names, sizes):
        out[n] = w_in[:, off:off + s]
        off += s
    assert off == w_in.shape[1]
    return out


def _gate_groups(cols, b_igate, b_fgate):
    w_gate = jnp.concatenate([cols["mi"], cols["mf"]], axis=1)
    bias = jnp.concatenate([b_igate, b_fgate]).astype(F32)
    pad_to = lambda a, n, axis: jnp.pad(a, [(0, n - a.shape[i]) if i == axis else (0, 0) for i in range(a.ndim)])
    return (_group(pad_to(w_gate, GATE_LANES, 1), False, "gate_col", auxv=pad_to(bias.reshape(1, -1), GATE_LANES, 1)),
            _group(pad_to(w_gate, GATE_ROWS, 1), True, "gate_row", auxv=pad_to(bias.reshape(-1, 1), GATE_ROWS, 0)))


def _mlstm_prompt_kernel(qT_ref, k_ref, vT_ref, oT_ref, gcol_ref, grow_ref, gm_ref,
                         h_ref, c_out_ref, n_out_ref, m_out_ref, c_s, n_s, m_s):
    step = pl.program_id(0)
    L = k_ref.shape[0]

    @pl.when(step == 0)
    def _():
        c_s[...] = jnp.zeros_like(c_s)
        n_s[...] = jnp.zeros_like(n_s)
        m_s[...] = jnp.zeros_like(m_s)

    r = lax.broadcasted_iota(jnp.int32, (L, L), 0)
    c = lax.broadcasted_iota(jnp.int32, (L, L), 1)
    tril = (c <= r).astype(F32)
    triu = (r <= c)
    gcol = gcol_ref[...]
    grow = grow_ref[0]
    bcol_all = jnp.dot(tril, gcol, precision=HIGHEST, preferred_element_type=F32)
    brow_all = jnp.dot(grow, triu.astype(F32), precision=HIGHEST, preferred_element_type=F32)

    for h in range(N_HEADS):
        sl = slice(h * HEAD_DIM, (h + 1) * HEAD_DIM)
        qTb = qT_ref[0, sl, :].astype(BF16)
        k = k_ref[:, sl]
        vTb = vT_ref[0, sl, :].astype(BF16)
        i_col = gcol[:, h:h + 1]
        b_col = bcol_all[:, N_HEADS + h:N_HEADS + h + 1]
        b_row = brow_all[N_HEADS + h:N_HEADS + h + 1, :]
        m_prev = m_s[h][0:1, 0:1]
        ct = c_s[h]
        n_row = n_s[h][0:1, :]

        g_col = i_col - b_col
        dT = jnp.where(triu, g_col + b_row, -jnp.inf)
        m_intra = jnp.max(dT, axis=0, keepdims=True)
        a_row = b_row + m_prev
        m_t = jnp.maximum(a_row, m_intra)
        wT = jnp.exp(dT - m_t)
        w_inter = jnp.exp(a_row - m_t)

        sT = _dot(k.astype(BF16), qTb)
        pT = sT * wT
        den = jnp.sum(pT, axis=0, keepdims=True)
        numT = _dot(vTb, pT.astype(BF16))
        cq = _dot(ct.astype(BF16), qTb)
        nq = _dot(jnp.broadcast_to(n_row, (8, HEAD_DIM)).astype(BF16), qTb)[0:1, :]
        numT = numT + w_inter * cq
        den = den + w_inter * nq
        hT = numT / jnp.maximum(jnp.abs(den), jnp.exp(-m_t))

        ms = jnp.mean(hT * hT, axis=0, keepdims=True)
        yT = hT * lax.rsqrt(ms + NORM_EPS) * gm_ref[sl, :] * oT_ref[0, sl, :]
        h_ref[:, sl] = yT.T

        m_new = m_t[:, L - 1:L]
        b_last = b_row[:, L - 1:L]
        decay = jnp.exp(b_last + m_prev - m_new)
        w_s = jnp.exp(b_last + g_col - m_new)
        kw = k * w_s
        c_new = decay * ct + _dot(vTb, kw.astype(BF16))
        n_new = decay * n_row + jnp.sum(kw, axis=0, keepdims=True)
        c_s[h] = c_new
        n_s[h] = jnp.broadcast_to(n_new, (8, HEAD_DIM))
        m_s[h] = jnp.broadcast_to(m_new, (8, HEAD_DIM))

    @pl.when(step == pl.num_programs(0) - 1)
    def _():
        for h in range(N_HEADS):
            c_out_ref[h] = c_s[h].T
        n_out_ref[...] = n_s[...]
        m_out_ref[...] = m_s[...]


def _mlstm_prompt(qT, k, vT, oT, gcol, grow, g_mhead):
    t = k.shape[0]
    L = qT.shape[2]
    assert t % L == 0 and qT.shape[0] * L == t
    blk_bytes = 3 * _nbytes((WIDTH, L), F32) + 2 * _nbytes((L, WIDTH), F32) + _nbytes((L, GATE_LANES), F32)
    state = jax.ShapeDtypeStruct((N_HEADS, 8, HEAD_DIM), F32)
    outs = pl.pallas_call(
        _mlstm_prompt_kernel,
        grid=(t // L,),
        in_specs=[pl.BlockSpec((1, WIDTH, L), lambda i: (i, 0, 0)),
                  pl.BlockSpec((L, WIDTH), lambda i: (i, 0)),
                  pl.BlockSpec((1, WIDTH, L), lambda i: (i, 0, 0)),
                  pl.BlockSpec((1, WIDTH, L), lambda i: (i, 0, 0)),
                  pl.BlockSpec((L, GATE_LANES), lambda i: (i, 0)),
                  pl.BlockSpec((1, GATE_ROWS, L), lambda i: (i, 0, 0)),
                  _const_spec((WIDTH, 1))],
        out_specs=[pl.BlockSpec((L, WIDTH), lambda i: (i, 0)),
                   pl.BlockSpec((N_HEADS, HEAD_DIM, HEAD_DIM), lambda i: (0, 0, 0)),
                   pl.BlockSpec((N_HEADS, 8, HEAD_DIM), lambda i: (0, 0, 0)),
                   pl.BlockSpec((N_HEADS, 8, HEAD_DIM), lambda i: (0, 0, 0))],
        out_shape=[jax.ShapeDtypeStruct((t, WIDTH), F32),
                   jax.ShapeDtypeStruct((N_HEADS, HEAD_DIM, HEAD_DIM), F32), state, state],
        scratch_shapes=[pltpu.VMEM((N_HEADS, HEAD_DIM, HEAD_DIM), F32),
                        pltpu.VMEM((N_HEADS, 8, HEAD_DIM), F32),
                        pltpu.VMEM((N_HEADS, 8, HEAD_DIM), F32)],
        compiler_params=_params(("arbitrary",), 2 * blk_bytes + 16 * _nbytes((L, L), F32)),
        name="mlstm_prompt",
    )(qT, k, vT, oT, gcol, grow, g_mhead.reshape(WIDTH, 1))
    hm, c_fin, n_fin, m_fin = outs
    return hm, c_fin, n_fin[:, 0, :], m_fin[:, 0, 0]


def _mlstm_sample_kernel(q_ref, k_ref, kT_ref, v_ref, o_ref, gcol_ref, grow_ref, c0_ref, n0_ref, m0_ref,
                         gm_ref, h_ref, c_out_ref, n_out_ref, m_out_ref):
    L = q_ref.shape[1]
    for b in range(q_ref.shape[0]):
        _mlstm_sample_one(b, L, q_ref, k_ref, kT_ref, v_ref, o_ref, gcol_ref, grow_ref, c0_ref, n0_ref, m0_ref,
                          gm_ref, h_ref, c_out_ref, n_out_ref, m_out_ref)


def _mlstm_sample_one(b, L, q_ref, k_ref, kT_ref, v_ref, o_ref, gcol_ref, grow_ref, c0_ref, n0_ref, m0_ref,
                      gm_ref, h_ref, c_out_ref, n_out_ref, m_out_ref):
    r = lax.broadcasted_iota(jnp.int32, (L, L), 0)
    c = lax.broadcasted_iota(jnp.int32, (L, L), 1)
    causal = c <= r
    gcol = gcol_ref[b]
    grow = grow_ref[b]
    bcol_all = jnp.dot(causal.astype(F32), gcol, precision=HIGHEST, preferred_element_type=F32)
    brow_all = jnp.dot(grow, (r <= c).astype(F32), precision=HIGHEST, preferred_element_type=F32)
    m0 = m0_ref[b]

    for h in range(N_HEADS):
        sl = slice(h * HEAD_DIM, (h + 1) * HEAD_DIM)
        q = q_ref[b][:, sl]
        k = k_ref[b][:, sl]
        kT = kT_ref[b][sl, :]
        v = v_ref[b][:, sl]
        i_col = gcol[:, h:h + 1]
        i_row = grow[h:h + 1, :]
        b_col = bcol_all[:, N_HEADS + h:N_HEADS + h + 1]
        b_row = brow_all[N_HEADS + h:N_HEADS + h + 1, :]
        m_prev = m0[h:h + 1, 0:1]
        c_old = c0_ref[b, h]
        n_old = n0_ref[b][h:h + 1, :]

        dm = jnp.where(causal, b_col - b_row + i_row, -jnp.inf)
        m_intra = jnp.max(dm, axis=1, keepdims=True)
        a_col = b_col + m_prev
        m_t = jnp.maximum(a_col, m_intra)
        w = jnp.exp(dm - m_t)
        w_inter = jnp.exp(a_col - m_t)

        s = _dot_nt(q, k) * w
        num = _dot(s, v) + w_inter * _dot(q.astype(BF16), c_old.astype(BF16))
        qn = jnp.sum(q * n_old, axis=1, keepdims=True)
        den = jnp.sum(s, axis=1, keepdims=True) + w_inter * qn
        hh = num / jnp.maximum(jnp.abs(den), jnp.exp(-m_t))
        ms = jnp.mean(hh * hh, axis=-1, keepdims=True)
        h_ref[b, :, sl] = hh * lax.rsqrt(ms + NORM_EPS) * gm_ref[:, sl] * o_ref[b][:, sl]

        m_new = m_t[L - 1:L, :]
        b_last = b_col[L - 1:L, :]
        decay = jnp.exp(b_last + m_prev - m_new)
        ws_row = jnp.exp(b_last - b_row + i_row - m_new)
        ws_col = jnp.exp(b_last - b_col + i_col - m_new)
        c_out_ref[b, h] = decay * c_old + _dot(kT * ws_row, v)
        n_out_ref[b, h:h + 1, :] = decay * n_old + jnp.sum(k * ws_col, axis=0, keepdims=True)
        m_out_ref[b, h:h + 1, :] = jnp.broadcast_to(m_new, (1, HEAD_DIM))


def _mlstm_sample(q, k, kT, v, o, gcol, grow, c0, n0, m0, g_mhead):
    db, L, _ = q.shape
    sb = SEQS_PER_STEP if db % SEQS_PER_STEP == 0 else 1
    seq = lambda *shape: pl.BlockSpec((sb,) + shape, lambda b: (b,) + (0,) * len(shape))
    m0_rep = jnp.broadcast_to(m0[:, :, None], (db, N_HEADS, HEAD_DIM))
    nbytes = sb * (2 * 2 * _nbytes((N_HEADS, HEAD_DIM, HEAD_DIM), F32) + 16 * _nbytes((8, WIDTH), F32))
    outs = pl.pallas_call(
        _mlstm_sample_kernel,
        grid=(db // sb,),
        in_specs=[seq(L, WIDTH), seq(L, WIDTH), seq(WIDTH, L), seq(L, WIDTH), seq(L, WIDTH),
                  seq(L, GATE_LANES), seq(GATE_ROWS, L),
                  seq(N_HEADS, HEAD_DIM, HEAD_DIM), seq(N_HEADS, HEAD_DIM), seq(N_HEADS, HEAD_DIM),
                  _const_spec((1, WIDTH))],
        out_specs=[seq(L, WIDTH), seq(N_HEADS, HEAD_DIM, HEAD_DIM), seq(N_HEADS, HEAD_DIM),
                   seq(N_HEADS, HEAD_DIM)],
        out_shape=[jax.ShapeDtypeStruct((db, L, WIDTH), F32),
                   jax.ShapeDtypeStruct((db, N_HEADS, HEAD_DIM, HEAD_DIM), F32),
                   jax.ShapeDtypeStruct((db, N_HEADS, HEAD_DIM), F32),
                   jax.ShapeDtypeStruct((db, N_HEADS, HEAD_DIM), F32)],
        compiler_params=_params(("parallel",), nbytes),
        name="mlstm_sample",
    )(q, k, kT, v, o, gcol, grow, c0, n0, m0_rep, g_mhead.reshape(1, WIDTH))
    hm, c_new, n_new, m_new = outs
    return hm, c_new, n_new, m_new[:, :, 0]


def _rel_bucket(dist):
    max_exact = REL_BUCKETS // 2
    n = jnp.maximum(dist, 0)
    nf = jnp.maximum(n, 1).astype(F32)
    large = max_exact + (jnp.log(nf / max_exact) / math.log(REL_MAX_DIST / max_exact)
                         * (REL_BUCKETS - max_exact)).astype(jnp.int32)
    large = jnp.minimum(large, REL_BUCKETS - 1)
    return jnp.where(n < max_exact, n, large)


def _bias_of_bucket(bucket, rb_ref, h):
    out = jnp.zeros(bucket.shape, F32)
    for b in range(REL_BUCKETS):
        out = jnp.where(bucket == b, rb_ref[b * N_HEADS + h], out)
    return out


def _rel_bias_from_dist(dist, rb_ref, h):
    return _bias_of_bucket(_rel_bucket(dist), rb_ref, h)


def _prompt_bias_kernel(rb_ref, o_ref):
    delta = pl.program_id(0)
    h = pl.program_id(1)
    s = lax.broadcasted_iota(jnp.int32, (MOBA_BLOCK, MOBA_BLOCK), 0)
    t = lax.broadcasted_iota(jnp.int32, (MOBA_BLOCK, MOBA_BLOCK), 1)
    dist = delta * MOBA_BLOCK + t - s
    bias = _rel_bias_from_dist(dist, rb_ref, h) - rb_ref[(REL_BUCKETS - 1) * N_HEADS + h]
    o_ref[0, 0] = jnp.where(dist >= 0, bias * LOG2E, NEG_INF)


def _near_blocks():
    return -(-(REL_SAT + MOBA_BLOCK - 1) // MOBA_BLOCK)


def _prompt_bias(rel_bias):
    nd = _near_blocks() + 1
    return pl.pallas_call(
        _prompt_bias_kernel,
        grid=(nd, N_HEADS),
        in_specs=[pl.BlockSpec(memory_space=pltpu.SMEM)],
        out_specs=pl.BlockSpec((1, 1, MOBA_BLOCK, MOBA_BLOCK), lambda d, h: (d, h, 0, 0)),
        out_shape=jax.ShapeDtypeStruct((nd, N_HEADS, MOBA_BLOCK, MOBA_BLOCK), F32),
        compiler_params=_params(("parallel", "parallel"), 4 * _nbytes((MOBA_BLOCK, MOBA_BLOCK), F32)),
        name="prompt_bias",
    )(rel_bias.reshape(-1))


def _top_mask(sc, valid, ksel, axis):
    n = sc.shape[axis]
    iota = lax.broadcasted_iota(jnp.int32, sc.shape, axis)
    cur = jnp.where(valid, sc, NEG_INF)
    sel = jnp.zeros(sc.shape, F32)
    for _ in range(ksel):
        mx = jnp.max(cur, axis=axis, keepdims=True)
        idx = jnp.min(jnp.where(cur == mx, iota, n), axis=axis, keepdims=True)
        hit = iota == idx
        sel = jnp.where(hit, 1.0, sel)
        cur = jnp.where(hit, -jnp.inf, cur)
    return jnp.where((sel > 0.0) & valid, 0.0, NEG_INF)


def _mask_rows(nb):
    return -(-nb // 16) * 16


def _prompt_select_kernel(ksel, kmean_ref, qT_ref, o_ref):
    tq = qT_ref.shape[2]
    nb = kmean_ref.shape[0]
    nbp = o_ref.shape[2] - HEAD_DIM
    q0 = pl.program_id(0) * tq
    own = (q0 + lax.broadcasted_iota(jnp.int32, (nb, tq), 1)) // MOBA_BLOCK
    blk = lax.broadcasted_iota(jnp.int32, (nb, tq), 0)
    for h in range(N_HEADS):
        sl = slice(h * HEAD_DIM, (h + 1) * HEAD_DIM)
        q = qT_ref[0, sl, :]
        sc = jnp.dot(kmean_ref[:, sl], q, precision=HIGHEST, preferred_element_type=F32)
        mask = jnp.where(blk == own, 0.0, _top_mask(sc, blk < own, ksel, 0))
        o_ref[0, h, 0:HEAD_DIM, :] = (q * (HEAD_DIM ** -0.5 * LOG2E)).astype(BF16)
        o_ref[0, h, HEAD_DIM:HEAD_DIM + nb, :] = mask.astype(BF16)
        if nbp > nb:
            o_ref[0, h, HEAD_DIM + nb:, :] = jnp.zeros((nbp - nb, tq), BF16)


def _prompt_select(kmean, qT):
    nb = kmean.shape[0]
    nt, _, tq = qT.shape
    ksel = min(MOBA_TOPK, nb - 1)
    rows = HEAD_DIM + _mask_rows(nb)
    return pl.pallas_call(
        functools.partial(_prompt_select_kernel, ksel),
        grid=(nt,),
        in_specs=[_const_spec((nb, WIDTH)), pl.BlockSpec((1, WIDTH, tq), lambda i: (i, 0, 0))],
        out_specs=pl.BlockSpec((1, N_HEADS, rows, tq), lambda i: (i, 0, 0, 0)),
        out_shape=jax.ShapeDtypeStruct((nt, N_HEADS, rows, tq), BF16),
        compiler_params=_params(("parallel",), 4 * _nbytes((WIDTH, tq), F32)),
        name="prompt_select",
    )(kmean, qT)


def _moba_prompt_kernel(n_bias, qa_ref, k_ref, vT_ref, bias_ref, o_ref):
    i = pl.program_id(1)
    g_blocks = KV_GROUP
    rows = g_blocks * MOBA_BLOCK
    qa = qa_ref[0, 0]
    n_groups = i // g_blocks + 1
    far_groups = jnp.maximum(i - (n_bias - 2), 0) // g_blocks

    def logits(g, near):
        start = pl.multiple_of(g * rows, rows)
        sT = _dot(k_ref[0, pl.ds(start, rows), :], qa)
        if near:
            sT = jnp.concatenate(
                [sT[b * MOBA_BLOCK:(b + 1) * MOBA_BLOCK]
                 + bias_ref[jnp.clip(i - (g * g_blocks + b), 0, n_bias - 1), 0] for b in range(g_blocks)], axis=0)
        return sT, jnp.max(sT, axis=0, keepdims=True)

    def fold(g, sT, mx, state):
        m, l, acc = state
        m_new = jnp.maximum(m, mx)
        alpha = jnp.exp2(m - m_new)
        l, acc = alpha * l, alpha * acc
        for b in range(g_blocks):
            p = jnp.exp2(sT[b * MOBA_BLOCK:(b + 1) * MOBA_BLOCK] - m_new)
            l = l + jnp.sum(p, axis=0, keepdims=True)
            acc = acc + _dot(vT_ref[g * g_blocks + b], p.astype(BF16))
        return m_new, l, acc

    def stage(next_near):
        def body(g, carry):
            m, l, acc, s_cur, mx_cur = carry
            nxt = logits(g + 1, next_near)
            return fold(g, s_cur, mx_cur, (m, l, acc)) + nxt
        return body

    tq = qa.shape[1]
    carry = (jnp.full((1, tq), NEG_INF, F32), jnp.zeros((1, tq), F32), jnp.zeros((HEAD_DIM, tq), F32)) + logits(0, True)
    carry = lax.fori_loop(0, far_groups - 1, stage(False), carry)
    carry = lax.fori_loop(jnp.maximum(far_groups - 1, 0), n_groups - 1, stage(True), carry)
    m, l, acc = fold(n_groups - 1, carry[3], carry[4], carry[:3])
    o_ref[...] = (acc / l).T


def _moba_prompt(qa, k_aug, vT_bf, bias):
    nb, _, ka, _ = qa.shape
    t = k_aug.shape[1]
    n_bias = bias.shape[0]
    assert t == nb * MOBA_BLOCK and nb % KV_GROUP == 0 and vT_bf.shape == (nb, WIDTH, MOBA_BLOCK)
    nbytes = (2 * (_nbytes((t, -(-ka // 128) * 128), BF16) + _nbytes((t, HEAD_DIM), BF16)
                   + _nbytes((n_bias, MOBA_BLOCK, MOBA_BLOCK), F32))
              + 12 * KV_GROUP * _nbytes((MOBA_BLOCK, MOBA_BLOCK), F32))
    return pl.pallas_call(
        functools.partial(_moba_prompt_kernel, n_bias),
        grid=(N_HEADS, nb),
        in_specs=[pl.BlockSpec((1, 1, ka, MOBA_BLOCK), lambda h, i: (i, h, 0, 0)),
                  pl.BlockSpec((1, t, ka), lambda h, i: (h, 0, 0)),
                  pl.BlockSpec((nb, HEAD_DIM, MOBA_BLOCK), lambda h, i: (0, h, 0)),
                  pl.BlockSpec((n_bias, 1, MOBA_BLOCK, MOBA_BLOCK), lambda h, i: (0, h, 0, 0))],
        out_specs=pl.BlockSpec((MOBA_BLOCK, HEAD_DIM), lambda h, i: (i, h)),
        out_shape=jax.ShapeDtypeStruct((nb * MOBA_BLOCK, WIDTH), F32),
        compiler_params=_params(("arbitrary", "arbitrary"), nbytes),
        name="moba_prompt",
    )(qa, k_aug, vT_bf, bias)


def _sample_bias_kernel(past_len, t_new, rb_ref, past_ref, page_ref, new_ref):
    page = pl.program_id(0)

    def table(n_cols, first_kpos, masked):
        shape = (N_HEADS * t_new, n_cols)
        r = lax.broadcasted_iota(jnp.int32, shape, 0)
        c = lax.broadcasted_iota(jnp.int32, shape, 1)
        tpos = past_len + r % t_new
        kpos = first_kpos + c // N_HEADS
        bucket = _rel_bucket(tpos - kpos)
        ok = (kpos >= (tpos // MOBA_BLOCK) * MOBA_BLOCK) & (kpos <= tpos) if masked else None
        out = jnp.full(shape, NEG_INF, F32)
        for h in range(N_HEADS):
            same = (r // t_new == h) & (c % N_HEADS == h)
            out = jnp.where(same if ok is None else same & ok, _bias_of_bucket(bucket, rb_ref, h) * LOG2E, out)
        return out

    past_ref[0] = table(PAGE_ROWS, page * PAGE_SIZE, False)
    page_ref[...] = table(PAGE_ROWS, past_len - PAGE_SIZE, True)
    new_ref[...] = table(new_ref.shape[1], past_len, True)


def _sample_bias(rel_bias, past_len, t_new):
    rows = N_HEADS * t_new
    n_pages = past_len // PAGE_SIZE
    new_cols = t_new * N_HEADS
    return pl.pallas_call(
        functools.partial(_sample_bias_kernel, past_len, t_new),
        grid=(n_pages,),
        in_specs=[pl.BlockSpec(memory_space=pltpu.SMEM)],
        out_specs=[pl.BlockSpec((1, rows, PAGE_ROWS), lambda p: (p, 0, 0)),
                   pl.BlockSpec((rows, PAGE_ROWS), lambda p: (0, 0)),
                   pl.BlockSpec((rows, new_cols), lambda p: (0, 0))],
        out_shape=[jax.ShapeDtypeStruct((n_pages, rows, PAGE_ROWS), F32),
                   jax.ShapeDtypeStruct((rows, PAGE_ROWS), F32),
                   jax.ShapeDtypeStruct((rows, new_cols), F32)],
        compiler_params=_params(("arbitrary",), 16 * _nbytes((rows, PAGE_ROWS), F32)),
        name="sample_bias",
    )(rel_bias.reshape(-1))


def _page_specs(pps, n_pages):
    def spec(u):
        return pl.BlockSpec((1, PAGE_ROWS, HEAD_DIM), lambda b, c, pt: (pt[b * n_pages + c * pps + u], 0, 0))
    return [spec(u) for u in range(pps)]


def _softmax_partial(logit_tiles, value_tiles):
    m = jnp.max(logit_tiles[0], axis=1, keepdims=True)
    for lg in logit_tiles[1:]:
        m = jnp.maximum(m, jnp.max(lg, axis=1, keepdims=True))
    l, o = 0.0, 0.0
    for lg, v in zip(logit_tiles, value_tiles):
        p = jnp.exp2(lg - m)
        l = l + jnp.sum(p, axis=1, keepdims=True)
        o = o + _dot(p.astype(BF16), v.astype(BF16))
    return m, l, o


def _moba_sample_kernel(pps, n_blocks, t_new, pt_ref, q_ref, knew_ref, vnew_ref,
                        bpast_ref, bpage_ref, bnew_ref, *refs):
    k_refs, v_refs = refs[:pps], refs[pps:2 * pps]
    o_ref = refs[2 * pps]
    sc_s, m_s, l_s, o_s = refs[2 * pps + 1:]
    del pt_ref
    c = pl.program_id(1)
    nch = pl.num_programs(1)
    ppb = MOBA_BLOCK // PAGE_SIZE
    bps = pps // ppb
    rows = N_HEADS * t_new
    fold = 8 // N_HEADS
    ksel = min(MOBA_TOPK, n_blocks)
    q = q_ref[0]
    qb = (q * (HEAD_DIM ** -0.5 * LOG2E)).astype(BF16)
    lanes = lambda a: jnp.broadcast_to(a, (rows, HEAD_DIM))

    pages = [k_refs[u][0] for u in range(pps)]
    tiles = [_dot_nt(qb, pages[u].astype(BF16)) + bpast_ref[c * pps + u] for u in range(pps)]
    for b in range(bps):
        n = c * bps + b
        ksum = sum(jnp.sum(pages[b * ppb + u].reshape(PAGE_ROWS // 8, 8, HEAD_DIM), axis=0)
                   for u in range(ppb))
        kmean = sum(ksum[f * N_HEADS:(f + 1) * N_HEADS] for f in range(fold)) / MOBA_BLOCK
        kmean_rows = jnp.concatenate(
            [jnp.broadcast_to(kmean[h:h + 1], (t_new, HEAD_DIM)) for h in range(N_HEADS)], axis=0)
        sc_s[n] = lanes(jnp.sum(q * kmean_rows, axis=1, keepdims=True))
        m, l, o = _softmax_partial(tiles[b * ppb:(b + 1) * ppb], [v_refs[b * ppb + u][0] for u in range(ppb)])
        m_s[n], l_s[n], o_s[n] = lanes(m), lanes(l), o

    @pl.when(c == nch - 1)
    def _():
        own_page = _dot_nt(qb, pages[pps - 1].astype(BF16)) + bpage_ref[...]
        m_pg, l_pg, o_pg = _softmax_partial([own_page], [v_refs[pps - 1][0]])
        new = _dot_nt(qb, knew_ref[0].astype(BF16)) + bnew_ref[...]
        m_nw, l_nw, o_nw = _softmax_partial([new], [vnew_ref[0]])

        lane = lax.broadcasted_iota(jnp.int32, (rows, HEAD_DIM), 1)
        r = lax.broadcasted_iota(jnp.int32, (rows, HEAD_DIM), 0)
        sc, mm, ll = jnp.zeros((rows, HEAD_DIM), F32), jnp.zeros((rows, HEAD_DIM), F32), jnp.zeros((rows, HEAD_DIM), F32)
        for n in range(n_blocks):
            at = lane == n
            sc, mm, ll = jnp.where(at, sc_s[n], sc), jnp.where(at, m_s[n], mm), jnp.where(at, l_s[n], ll)
        tpos = n_blocks * MOBA_BLOCK + r % t_new
        valid = (lane < tpos // MOBA_BLOCK) & (lane < n_blocks)
        mm = jnp.where(lane < n_blocks, mm + _top_mask(sc, valid, ksel, 1), NEG_INF)
        m_all = jnp.maximum(jnp.max(mm, axis=1, keepdims=True), jnp.maximum(m_pg, m_nw))
        w = jnp.exp2(mm - m_all)
        w_pg, w_nw = jnp.exp2(m_pg - m_all), jnp.exp2(m_nw - m_all)
        den = jnp.sum(w * ll, axis=1, keepdims=True) + w_pg * l_pg + w_nw * l_nw
        num = w_pg * o_pg + w_nw * o_nw
        for n in range(n_blocks):
            num = num + jnp.sum(jnp.where(lane == n, w, 0.0), axis=1, keepdims=True) * o_s[n]
        o_ref[0] = num / den


def _moba_sample(q_rows, k_new, v_new, cache_k, cache_v, page_table, bias_past, bias_page, bias_new):
    db, rows, _ = q_rows.shape
    t_new = rows // N_HEADS
    n_pages = page_table.shape[1]
    ppb = MOBA_BLOCK // PAGE_SIZE
    assert n_pages % ppb == 0 and 8 % N_HEADS == 0 and t_new % 8 == 0
    n_blocks = n_pages // ppb
    assert n_blocks <= HEAD_DIM, "per-block scalars are gathered into one lane tile"
    pps = min(PAGES_PER_STEP, n_pages)
    assert n_pages % pps == 0 and pps % ppb == 0
    seq = lambda *shape: pl.BlockSpec((1,) + shape, lambda b, c, pt: (b,) + (0,) * len(shape))
    slab = pltpu.VMEM((n_blocks, rows, HEAD_DIM), F32)
    nbytes = (4 * pps * _nbytes((PAGE_ROWS, HEAD_DIM), F32) + _nbytes(bias_past.shape, F32)
              + 4 * _nbytes((n_blocks, rows, HEAD_DIM), F32) + 64 * _nbytes((rows, PAGE_ROWS), F32))
    return pl.pallas_call(
        functools.partial(_moba_sample_kernel, pps, n_blocks, t_new),
        grid_spec=pltpu.PrefetchScalarGridSpec(
            num_scalar_prefetch=1,
            grid=(db, n_pages // pps),
            in_specs=[seq(rows, HEAD_DIM), seq(rows, HEAD_DIM), seq(rows, HEAD_DIM),
                      _const_spec(bias_past.shape), _const_spec(bias_page.shape), _const_spec(bias_new.shape)]
                     + _page_specs(pps, n_pages) + _page_specs(pps, n_pages),
            out_specs=seq(rows, HEAD_DIM),
            scratch_shapes=[slab, slab, slab, slab]),
        out_shape=jax.ShapeDtypeStruct((db, rows, HEAD_DIM), F32),
        compiler_params=_params(("parallel", "arbitrary"), nbytes),
        name="moba_sample",
    )(page_table.reshape(-1), q_rows, k_new, v_new, bias_past, bias_page, bias_new,
      *([cache_k] * pps), *([cache_v] * pps))


def _out_ffn_kernel(ff_chunk, x_ref, hm_ref, ha_ref, sgm_ref, sga_ref, gate1_ref, shift2_ref, scale2_ref,
                    gate2_ref, g_ref, wbm_ref, wba_ref, wout_ref, wup_ref, wdn_ref, y_ref):
    bm = _dot(hm_ref[...].astype(BF16), wbm_ref[...])
    ba = _dot(ha_ref[...].astype(BF16), wba_ref[...])
    mix = sgm_ref[...] * bm + sga_ref[...] * ba
    x1 = x_ref[...] + gate1_ref[...] * _dot(mix.astype(BF16), wout_ref[...])
    y = x1 * lax.rsqrt(jnp.mean(x1 * x1, axis=-1, keepdims=True) + NORM_EPS) * g_ref[...]
    h2 = (y * (1.0 + scale2_ref[...]) + shift2_ref[...]).astype(BF16)
    d_ff = wup_ref.shape[1]
    acc = jnp.zeros(x1.shape, F32)
    for c in range(d_ff // ff_chunk):
        u = jnp.maximum(_dot(h2, wup_ref[:, c * ff_chunk:(c + 1) * ff_chunk]), 0.0)
        acc = acc + _dot((u * u).astype(BF16), wdn_ref[c * ff_chunk:(c + 1) * ff_chunk, :])
    y_ref[...] = x1 + gate2_ref[...] * acc


def _out_ffn(x, hm, ha, sgm, sga, gate1, shift2, scale2, gate2, g_norm, wbm, wba, wout, wup, wdn, tm):
    t, d = x.shape
    per_tok = gate1.shape[0] != 1
    tok = lambda n: pl.BlockSpec((tm, n), lambda i: (i, 0))
    mod = tok(d) if per_tok else pl.BlockSpec((1, d), lambda i: (0, 0))
    weights = (wbm, wba, wout, wup, wdn)
    nbytes = (sum(_nbytes(w.shape, w.dtype) for w in weights)
              + 2 * _nbytes((tm, d), F32) * (5 + (4 if per_tok else 0)) + 8 * _nbytes((tm, d), F32))
    return pl.pallas_call(
        functools.partial(_out_ffn_kernel, 1024),
        grid=(t // tm,),
        in_specs=[tok(d), tok(WIDTH), tok(WIDTH), tok(d), tok(d), mod, mod, mod, mod, _const_spec((1, d))]
                 + [_const_spec(w.shape) for w in weights],
        out_specs=tok(d),
        out_shape=jax.ShapeDtypeStruct((t, d), F32),
        compiler_params=_params(("parallel",), nbytes),
        name="out_ffn",
    )(x, hm, ha, sgm, sga, gate1, shift2, scale2, gate2, g_norm.reshape(1, d), *weights)


def _layer(x_prompt, x_sample, cache_k, cache_v, state_c, state_n, state_m, page_table, c_prompt, c_sample,
           rel_bias, w_ada, b_ada, g_norm_mix, w_in, b_igate, b_fgate, g_mhead, g_qnorm, g_knorm,
           w_branch_m, w_branch_a, w_out, g_norm_ffn, w_ff_up, w_ff_down):
    bsz, seq, d = x_prompt.shape
    db, t_new, _ = x_sample.shape
    assert bsz == 1, "the prompt path handles one sequence"
    n_pages = page_table.shape[1]
    past_len = n_pages * PAGE_SIZE
    k_scale = HEAD_DIM ** -0.5

    c_all = jnp.concatenate([c_prompt, c_sample], axis=0)
    pad = (-c_all.shape[0]) % 8
    mod = _ada(jnp.pad(c_all, ((0, pad), (0, 0))), w_ada, b_ada)
    mod_p = jnp.split(mod[0:1], N_COND, axis=-1)
    mod_s = jnp.split(jnp.repeat(mod[1:1 + db], t_new, axis=0), N_COND, axis=-1)

    cols = _split_w_in(w_in)
    gate_col, gate_row = _gate_groups(cols, b_igate, b_fgate)
    gq_row, gk_row = g_qnorm.reshape(1, WIDTH), g_knorm.reshape(1, WIDTH)
    gq_col = g_qnorm.reshape(WIDTH, 1)
    wbm, wba, wout = w_branch_m.astype(BF16), w_branch_a.astype(BF16), w_out.astype(BF16)
    wup, wdn = w_ff_up.astype(BF16), w_ff_down.astype(BF16)

    plan_p = [
        _group(cols["mq"], True), _group(cols["mk"], False, "scale", const=k_scale), _group(cols["mv"], True),
        _group(cols["mo"], True, "sigmoid"), gate_col, gate_row,
        _group(cols["aq"], True, "norm_col", auxv=gq_col),
        _group(cols["ak"], False, "norm_row", auxv=gk_row, outs=("heads", BF16, "mean")),
        _group(cols["av"], False, outs=("heads",)), _group(cols["av"], True, outs=(BF16,)),
        _group(cols["gm"], False, "sigmoid"), _group(cols["ga"], False, "sigmoid"),
    ]
    xp = x_prompt.reshape(seq, d)
    (mqT, mk, mvT, soT, gcol, grow, aqT, ak, ak_bf, kmean, av, avT_bf, sgm, sga) = _in_proj(
        xp, mod_p[0], mod_p[1], g_norm_mix, plan_p, TOKEN_TILE)
    hm, c_p, n_p, m_p = _mlstm_prompt(mqT, mk, mvT, soT, gcol, grow, g_mhead)
    nb = seq // MOBA_BLOCK
    qa = _prompt_select(kmean.reshape(nb, WIDTH), aqT)
    block_id = jnp.arange(seq, dtype=jnp.int32)[:, None] // MOBA_BLOCK
    one_hot = (block_id == jnp.arange(_mask_rows(nb), dtype=jnp.int32)[None, :]).astype(BF16)
    k_aug = jnp.concatenate([ak_bf.reshape(seq, N_HEADS, HEAD_DIM).transpose(1, 0, 2),
                             jnp.broadcast_to(one_hot, (N_HEADS,) + one_hot.shape)], axis=-1)
    ha = _moba_prompt(qa, k_aug, avT_bf, _prompt_bias(rel_bias))
    y_p = _out_ffn(xp, hm, ha, sgm, sga, mod_p[2], mod_p[3], mod_p[4], mod_p[5], g_norm_ffn,
                   wbm, wba, wout, wup, wdn, TOKEN_TILE)

    plan_s = [
        _group(cols["mq"], False), _group(cols["mk"], False, "scale", const=k_scale), _group(cols["mv"], False),
        _group(cols["mo"], False, "sigmoid"), gate_col, gate_row,
        _group(cols["aq"], False, "norm_row", auxv=gq_row),
        _group(cols["ak"], False, "norm_row", auxv=gk_row, outs=("heads",)),
        _group(cols["av"], False, outs=("heads",)),
        _group(cols["gm"], False, "sigmoid"), _group(cols["ga"], False, "sigmoid"),
    ]
    ts = db * t_new
    xs = x_sample.reshape(ts, d)
    (mq_s, mk_s, mv_s, so_s, gcol_s, grow_s, aq_s, ak_s, av_s, sgm_s, sga_s) = _in_proj(
        xs, mod_s[0], mod_s[1], g_norm_mix, plan_s, min(TOKEN_TILE, ts))
    per_seq = lambda a: a.reshape(db, t_new, a.shape[-1])
    mk_seq = per_seq(mk_s)
    grow_seq = jnp.transpose(grow_s, (1, 0, 2)).reshape(GATE_ROWS, db, t_new).transpose(1, 0, 2)
    hm_s, c_s, n_s, m_s = _mlstm_sample(per_seq(mq_s), mk_seq, jnp.swapaxes(mk_seq, 1, 2), per_seq(mv_s),
                                        per_seq(so_s), per_seq(gcol_s), grow_seq,
                                        state_c, state_n, state_m, g_mhead)
    rows = N_HEADS * t_new
    ck = cache_k.reshape(cache_k.shape[0], PAGE_ROWS, HEAD_DIM)
    cv = cache_v.reshape(cache_v.shape[0], PAGE_ROWS, HEAD_DIM)
    head_major = lambda a: a.reshape(db, t_new, N_HEADS, HEAD_DIM).transpose(0, 2, 1, 3).reshape(db, rows, HEAD_DIM)
    b_past, b_page, b_new = _sample_bias(rel_bias, past_len, t_new)
    ha_rows = _moba_sample(head_major(aq_s), ak_s.reshape(db, rows, HEAD_DIM), av_s.reshape(db, rows, HEAD_DIM),
                           ck, cv, page_table, b_past, b_page, b_new)
    ha_s = ha_rows.reshape(db, N_HEADS, t_new, HEAD_DIM).transpose(0, 2, 1, 3)
    y_s = _out_ffn(xs, hm_s.reshape(ts, WIDTH), ha_s.reshape(ts, WIDTH), sgm_s, sga_s,
                   mod_s[2], mod_s[3], mod_s[4], mod_s[5], g_norm_ffn, wbm, wba, wout, wup, wdn,
                   min(TOKEN_TILE, ts))

    heads = lambda a, lead: a.reshape(lead + (N_HEADS, HEAD_DIM))
    return (y_p.reshape(bsz, seq, d), y_s.reshape(db, t_new, d),
            heads(ak, (bsz, seq)), heads(av, (bsz, seq)), c_p[None], n_p[None], m_p[None],
            heads(ak_s, (db, t_new)), heads(av_s, (db, t_new)), c_s, n_s, m_s)


def kernel(x_prompt, x_sample, cache_k, cache_v, state_C, state_n, state_m, page_table, c_prompt, c_sample,
           rel_bias, w_ada, b_ada, g_norm_mix, w_in, b_igate, b_fgate, g_mhead, g_qnorm, g_knorm,
           w_branch_m, w_branch_a, w_out, g_norm_ffn, w_ff_up, w_ff_down):
    depth = w_in.shape[0]
    assert depth == 1, "state outputs are stacked per layer; this kernel implements the single-layer trunk"
    outs = _layer(x_prompt, x_sample, cache_k[0], cache_v[0], state_C[0], state_n[0], state_m[0], page_table,
                  c_prompt, c_sample, rel_bias, w_ada[0], b_ada[0], g_norm_mix[0], w_in[0], b_igate[0],
                  b_fgate[0], g_mhead[0], g_qnorm[0], g_knorm[0], w_branch_m[0], w_branch_a[0], w_out[0],
                  g_norm_ffn[0], w_ff_up[0], w_ff_down[0])
    yp, ys, kp, vp, cp, np_, mp, ks, vs, cs, ns, ms = outs
    stack = lambda a: a[None]
    return (yp, ys, stack(kp), stack(vp), stack(cp), stack(np_), stack(mp),
            stack(ks), stack(vs), stack(cs), stack(ns), stack(ms))
```

```python
import functools
import math

import jax
import jax.numpy as jnp
from jax import lax
from jax.experimental import pallas as pl
from jax.experimental.pallas import tpu as pltpu

F32 = jnp.float32
BF16 = jnp.bfloat16
HIGHEST = lax.Precision.HIGHEST

N_HEADS = 4
HEAD_DIM = 128
WIDTH = N_HEADS * HEAD_DIM
MOBA_BLOCK = 256
MOBA_TOPK = 3
PAGE_SIZE = 128
REL_BUCKETS = 32
REL_MAX_DIST = 4096
N_COND = 6
NORM_EPS = 1e-6
NEG_INF = -1e30
LOG2E = math.log2(math.e)
GATE_LANES = 128
GATE_ROWS = 16

V7X_VMEM_BYTES = 64 * 1024 * 1024
VMEM_HEADROOM_BYTES = 8 * 1024 * 1024
TOKEN_TILE = 256
assert TOKEN_TILE == MOBA_BLOCK, "feature-major projection tiles double as MoBA key/query blocks"
PAGES_PER_STEP = 32
KV_GROUP = 4
SEQS_PER_STEP = 2
PAGE_ROWS = PAGE_SIZE * N_HEADS

REL_SAT = 3072


def _vmem_limit(nbytes):
    return int(min(V7X_VMEM_BYTES - VMEM_HEADROOM_BYTES, nbytes + VMEM_HEADROOM_BYTES))


def _nbytes(shape, dtype):
    return math.prod(shape) * jnp.dtype(dtype).itemsize


def _params(sem, nbytes):
    return pltpu.CompilerParams(dimension_semantics=sem, vmem_limit_bytes=_vmem_limit(nbytes))


def _const_spec(shape):
    nd = len(shape)
    return pl.BlockSpec(shape, lambda *_: (0,) * nd, pipeline_mode=pl.Buffered(1))


def _dot(a, b):
    return jnp.dot(a, b, preferred_element_type=F32)


def _dot_nt(a, b):
    return lax.dot_general(a, b, (((1,), (1,)), ((), ())), preferred_element_type=F32)


def _log_sigmoid(x):
    return -(jnp.maximum(-x, 0.0) + jnp.log1p(jnp.exp(-jnp.abs(x))))


def _ada_kernel(c_ref, w_ref, b_ref, o_ref):
    c = c_ref[...]
    s = (c * jax.nn.sigmoid(c)).astype(BF16)
    o_ref[...] = _dot(s, w_ref[...].astype(BF16)) + b_ref[...]


def _ada(c_all, w_ada, b_ada):
    rows, d = c_all.shape
    n_out = w_ada.shape[1]
    tn = d
    nbytes = 2 * (_nbytes((rows, d), F32) + _nbytes((d, tn), F32) + _nbytes((rows, tn), F32))
    return pl.pallas_call(
        _ada_kernel,
        grid=(n_out // tn,),
        in_specs=[pl.BlockSpec((rows, d), lambda j: (0, 0)),
                  pl.BlockSpec((d, tn), lambda j: (0, j)),
                  pl.BlockSpec((1, tn), lambda j: (0, j))],
        out_specs=pl.BlockSpec((rows, tn), lambda j: (0, j)),
        out_shape=jax.ShapeDtypeStruct((rows, n_out), F32),
        compiler_params=_params(("arbitrary",), nbytes),
        name="ada",
    )(c_all, w_ada, b_ada.reshape(1, n_out))


def _head_norm_rows(z, g):
    outs = []
    for h in range(N_HEADS):
        blk = z[:, h * HEAD_DIM:(h + 1) * HEAD_DIM]
        ms = jnp.mean(blk * blk, axis=-1, keepdims=True)
        outs.append(blk * lax.rsqrt(ms + NORM_EPS) * g[:, h * HEAD_DIM:(h + 1) * HEAD_DIM])
    return jnp.concatenate(outs, axis=-1)


def _head_norm_cols(z, g):
    outs = []
    for h in range(N_HEADS):
        blk = z[h * HEAD_DIM:(h + 1) * HEAD_DIM, :]
        ms = jnp.mean(blk * blk, axis=0, keepdims=True)
        outs.append(blk * lax.rsqrt(ms + NORM_EPS) * g[h * HEAD_DIM:(h + 1) * HEAD_DIM, :])
    return jnp.concatenate(outs, axis=0)


def _in_proj_kernel(plan, x_ref, shift_ref, scale_ref, g_ref, *refs):
    n_groups = len(plan)
    w_refs = refs[:n_groups]
    n_aux = sum(1 for p in plan if p["aux"])
    aux_refs = refs[n_groups:n_groups + n_aux]
    out_refs = refs[n_groups + n_aux:]

    x = x_ref[...]
    y = x * lax.rsqrt(jnp.mean(x * x, axis=-1, keepdims=True) + NORM_EPS) * g_ref[...]
    hb = (y * (1.0 + scale_ref[...]) + shift_ref[...]).astype(BF16)

    ai = 0
    oi = 0
    for p, w_ref in zip(plan, w_refs):
        z = _dot_nt(w_ref[...], hb) if p["trans"] else _dot(hb, w_ref[...])
        aux = None
        if p["aux"]:
            aux = aux_refs[ai][...]
            ai += 1
        epi = p["epi"]
        if epi == "scale":
            z = z * p["const"]
        elif epi == "sigmoid":
            z = jax.nn.sigmoid(z)
        elif epi == "norm_row":
            z = _head_norm_rows(z, aux)
        elif epi == "norm_col":
            z = _head_norm_cols(z, aux)
        elif epi == "gate_col":
            pre = z + aux
            lane = lax.broadcasted_iota(jnp.int32, pre.shape, 1)
            z = jnp.where(lane < N_HEADS, pre, _log_sigmoid(pre))
        elif epi == "gate_row":
            pre = z + aux
            row = lax.broadcasted_iota(jnp.int32, pre.shape, 0)
            z = jnp.where(row < N_HEADS, pre, _log_sigmoid(pre))
        for kind in p["outs"]:
            if kind == "heads":
                for h in range(N_HEADS):
                    out_refs[oi][:, h, :] = z[:, h * HEAD_DIM:(h + 1) * HEAD_DIM]
            elif kind == "mean":
                out_refs[oi][0] = jnp.mean(z, axis=0, keepdims=True)
            elif p["trans"]:
                out_refs[oi][0] = z.astype(kind)
            else:
                out_refs[oi][...] = z.astype(kind)
            oi += 1


def _in_proj(x, shift, scale, g_norm, plan, tm):
    t, d = x.shape
    per_tok = shift.shape[0] != 1
    mod_spec = (pl.BlockSpec((tm, d), lambda i: (i, 0)) if per_tok
                else pl.BlockSpec((1, d), lambda i: (0, 0)))
    in_specs = [pl.BlockSpec((tm, d), lambda i: (i, 0)), mod_spec, mod_spec, _const_spec((1, d))]
    args = [x, shift, scale, g_norm.reshape(1, d)]
    nbytes = 2 * _nbytes((tm, d), F32) * (3 if per_tok else 1)
    for p in plan:
        in_specs.append(_const_spec(p["w"].shape))
        args.append(p["w"])
        nbytes += _nbytes(p["w"].shape, p["w"].dtype)
    for p in plan:
        if p["aux"]:
            in_specs.append(_const_spec(p["auxv"].shape))
            args.append(p["auxv"])
    out_specs, out_shapes = [], []
    for p in plan:
        n = p["w"].shape[0] if p["trans"] else p["w"].shape[1]
        for kind in p["outs"]:
            if kind == "heads":
                assert not p["trans"] and n == WIDTH
                out_specs.append(pl.BlockSpec((tm, N_HEADS, HEAD_DIM), lambda i: (i, 0, 0)))
                out_shapes.append(jax.ShapeDtypeStruct((t, N_HEADS, HEAD_DIM), F32))
            elif kind == "mean":
                assert not p["trans"] and tm == MOBA_BLOCK
                out_specs.append(pl.BlockSpec((1, 1, n), lambda i: (i, 0, 0)))
                out_shapes.append(jax.ShapeDtypeStruct((t // tm, 1, n), F32))
            elif p["trans"]:
                out_specs.append(pl.BlockSpec((1, n, tm), lambda i: (i, 0, 0)))
                out_shapes.append(jax.ShapeDtypeStruct((t // tm, n, tm), kind))
            else:
                out_specs.append(pl.BlockSpec((tm, n), lambda i: (i, 0)))
                out_shapes.append(jax.ShapeDtypeStruct((t, n), kind))
            nbytes += 2 * _nbytes((tm, n), F32)
    plan_static = tuple({k: v for k, v in p.items() if k not in ("w", "auxv")} for p in plan)
    return pl.pallas_call(
        functools.partial(_in_proj_kernel, plan_static),
        grid=(t // tm,),
        in_specs=in_specs,
        out_specs=out_specs,
        out_shape=out_shapes,
        compiler_params=_params(("parallel",), nbytes + 4 * _nbytes((tm, 1024), F32)),
        name="in_proj",
    )(*args)


def _group(w, trans, epi="none", const=None, auxv=None, outs=(F32,)):
    w = (w.T if trans else w).astype(BF16)
    return dict(w=w, trans=trans, epi=epi, const=const, aux=auxv is not None, auxv=auxv, outs=tuple(outs))


def _split_w_in(w_in):
    sizes = (WIDTH, WIDTH, WIDTH, WIDTH, N_HEADS, N_HEADS, WIDTH, WIDTH, WIDTH, w_in.shape[0], w_in.shape[0])
    names = ("mq", "mk", "mv", "mo", "mi", "mf", "aq", "ak", "av", "gm", "ga")
    out, off = {}, 0
    for n, s in zip(names, sizes):
        out[n] = w_in[:, off:off + s]
        off += s
    assert off == w_in.shape[1]
    return out


def _gate_groups(cols, b_igate, b_fgate):
    w_gate = jnp.concatenate([cols["mi"], cols["mf"]], axis=1)
    bias = jnp.concatenate([b_igate, b_fgate]).astype(F32)
    pad_to = lambda a, n, axis: jnp.pad(a, [(0, n - a.shape[i]) if i == axis else (0, 0) for i in range(a.ndim)])
    return (_group(pad_to(w_gate, GATE_LANES, 1), False, "gate_col", auxv=pad_to(bias.reshape(1, -1), GATE_LANES, 1)),
            _group(pad_to(w_gate, GATE_ROWS, 1), True, "gate_row", auxv=pad_to(bias.reshape(-1, 1), GATE_ROWS, 0)))


def _mlstm_prompt_kernel(qT_ref, k_ref, vT_ref, oT_ref, gcol_ref, grow_ref, gm_ref,
                         h_ref, c_out_ref, n_out_ref, m_out_ref, c_s, n_s, m_s):
    step = pl.program_id(0)
    L = k_ref.shape[0]

    @pl.when(step == 0)
    def _():
        c_s[...] = jnp.zeros_like(c_s)
        n_s[...] = jnp.zeros_like(n_s)
        m_s[...] = jnp.zeros_like(m_s)

    r = lax.broadcasted_iota(jnp.int32, (L, L), 0)
    c = lax.broadcasted_iota(jnp.int32, (L, L), 1)
    tril = (c <= r).astype(F32)
    triu = (r <= c)
    gcol = gcol_ref[...]
    grow = grow_ref[0]
    bcol_all = jnp.dot(tril, gcol, precision=HIGHEST, preferred_element_type=F32)
    brow_all = jnp.dot(grow, triu.astype(F32), precision=HIGHEST, preferred_element_type=F32)

    hs = range(N_HEADS)
    sl = lambda h: slice(h * HEAD_DIM, (h + 1) * HEAD_DIM)
    qTb = [qT_ref[0, sl(h), :].astype(BF16) for h in hs]
    k = [k_ref[:, sl(h)] for h in hs]
    vTb = [vT_ref[0, sl(h), :].astype(BF16) for h in hs]
    m_prev = [m_s[h][0:1, 0:1] for h in hs]
    ct = [c_s[h] for h in hs]
    n_row = [n_s[h][0:1, :] for h in hs]
    b_row = [brow_all[N_HEADS + h:N_HEADS + h + 1, :] for h in hs]
    g_col = [gcol[:, h:h + 1] - bcol_all[:, N_HEADS + h:N_HEADS + h + 1] for h in hs]

    sT = [_dot(k[h].astype(BF16), qTb[h]) for h in hs]
    cq = [_dot(ct[h].astype(BF16), qTb[h]) for h in hs]
    nq = [_dot(jnp.broadcast_to(n_row[h], (8, HEAD_DIM)).astype(BF16), qTb[h])[0:1, :] for h in hs]
    dT = [jnp.where(triu, g_col[h] + b_row[h], -jnp.inf) for h in hs]
    a_row = [b_row[h] + m_prev[h] for h in hs]
    m_t = [jnp.maximum(a_row[h], jnp.max(dT[h], axis=0, keepdims=True)) for h in hs]
    pT = [sT[h] * jnp.exp(dT[h] - m_t[h]) for h in hs]
    w_inter = [jnp.exp(a_row[h] - m_t[h]) for h in hs]
    numT = [_dot(vTb[h], pT[h].astype(BF16)) + w_inter[h] * cq[h] for h in hs]
    den = [jnp.sum(pT[h], axis=0, keepdims=True) + w_inter[h] * nq[h] for h in hs]
    hT = [numT[h] / jnp.maximum(jnp.abs(den[h]), jnp.exp(-m_t[h])) for h in hs]
    ms = [jnp.mean(hT[h] * hT[h], axis=0, keepdims=True) for h in hs]
    for h in hs:
        yT = hT[h] * lax.rsqrt(ms[h] + NORM_EPS) * gm_ref[sl(h), :] * oT_ref[0, sl(h), :]
        h_ref[:, sl(h)] = yT.T

    m_new = [m_t[h][:, L - 1:L] for h in hs]
    b_last = [b_row[h][:, L - 1:L] for h in hs]
    decay = [jnp.exp(b_last[h] + m_prev[h] - m_new[h]) for h in hs]
    kw = [k[h] * jnp.exp(b_last[h] + g_col[h] - m_new[h]) for h in hs]
    c_new = [decay[h] * ct[h] + _dot(vTb[h], kw[h].astype(BF16)) for h in hs]
    for h in hs:
        c_s[h] = c_new[h]
        n_s[h] = jnp.broadcast_to(decay[h] * n_row[h] + jnp.sum(kw[h], axis=0, keepdims=True), (8, HEAD_DIM))
        m_s[h] = jnp.broadcast_to(m_new[h], (8, HEAD_DIM))

    @pl.when(step == pl.num_programs(0) - 1)
    def _():
        for h in range(N_HEADS):
            c_out_ref[h] = c_s[h].T
        n_out_ref[...] = n_s[...]
        m_out_ref[...] = m_s[...]


def _mlstm_prompt(qT, k, vT, oT, gcol, grow, g_mhead):
    t = k.shape[0]
    L = qT.shape[2]
    assert t % L == 0 and qT.shape[0] * L == t
    blk_bytes = 3 * _nbytes((WIDTH, L), F32) + 2 * _nbytes((L, WIDTH), F32) + _nbytes((L, GATE_LANES), F32)
    state = jax.ShapeDtypeStruct((N_HEADS, 8, HEAD_DIM), F32)
    outs = pl.pallas_call(
        _mlstm_prompt_kernel,
        grid=(t // L,),
        in_specs=[pl.BlockSpec((1, WIDTH, L), lambda i: (i, 0, 0)),
                  pl.BlockSpec((L, WIDTH), lambda i: (i, 0)),
                  pl.BlockSpec((1, WIDTH, L), lambda i: (i, 0, 0)),
                  pl.BlockSpec((1, WIDTH, L), lambda i: (i, 0, 0)),
                  pl.BlockSpec((L, GATE_LANES), lambda i: (i, 0)),
                  pl.BlockSpec((1, GATE_ROWS, L), lambda i: (i, 0, 0)),
                  _const_spec((WIDTH, 1))],
        out_specs=[pl.BlockSpec((L, WIDTH), lambda i: (i, 0)),
                   pl.BlockSpec((N_HEADS, HEAD_DIM, HEAD_DIM), lambda i: (0, 0, 0)),
                   pl.BlockSpec((N_HEADS, 8, HEAD_DIM), lambda i: (0, 0, 0)),
                   pl.BlockSpec((N_HEADS, 8, HEAD_DIM), lambda i: (0, 0, 0))],
        out_shape=[jax.ShapeDtypeStruct((t, WIDTH), F32),
                   jax.ShapeDtypeStruct((N_HEADS, HEAD_DIM, HEAD_DIM), F32), state, state],
        scratch_shapes=[pltpu.VMEM((N_HEADS, HEAD_DIM, HEAD_DIM), F32),
                        pltpu.VMEM((N_HEADS, 8, HEAD_DIM), F32),
                        pltpu.VMEM((N_HEADS, 8, HEAD_DIM), F32)],
        compiler_params=_params(("arbitrary",), 2 * blk_bytes + 16 * _nbytes((L, L), F32)),
        name="mlstm_prompt",
    )(qT, k, vT, oT, gcol, grow, g_mhead.reshape(WIDTH, 1))
    hm, c_fin, n_fin, m_fin = outs
    return hm, c_fin, n_fin[:, 0, :], m_fin[:, 0, 0]


def _mlstm_sample_kernel(q_ref, k_ref, kT_ref, v_ref, o_ref, gcol_ref, grow_ref, c0_ref, n0_ref, m0_ref,
                         gm_ref, h_ref, c_out_ref, n_out_ref, m_out_ref):
    L = q_ref.shape[1]
    r = lax.broadcasted_iota(jnp.int32, (L, L), 0)
    c = lax.broadcasted_iota(jnp.int32, (L, L), 1)
    causal = c <= r
    items = [(b, h) for b in range(q_ref.shape[0]) for h in range(N_HEADS)]
    sl = lambda h: slice(h * HEAD_DIM, (h + 1) * HEAD_DIM)
    each = lambda f: [f(b, h) for b, h in items]

    gcol = [gcol_ref[b] for b in range(q_ref.shape[0])]
    grow = [grow_ref[b] for b in range(q_ref.shape[0])]
    bcol = [jnp.dot(causal.astype(F32), g, precision=HIGHEST, preferred_element_type=F32) for g in gcol]
    brow = [jnp.dot(g, (r <= c).astype(F32), precision=HIGHEST, preferred_element_type=F32) for g in grow]

    q = each(lambda b, h: q_ref[b][:, sl(h)])
    k = each(lambda b, h: k_ref[b][:, sl(h)])
    v = each(lambda b, h: v_ref[b][:, sl(h)])
    c_old = each(lambda b, h: c0_ref[b, h])
    n_old = each(lambda b, h: n0_ref[b][h:h + 1, :])
    i_col = each(lambda b, h: gcol[b][:, h:h + 1])
    i_row = each(lambda b, h: grow[b][h:h + 1, :])
    b_col = each(lambda b, h: bcol[b][:, N_HEADS + h:N_HEADS + h + 1])
    b_row = each(lambda b, h: brow[b][N_HEADS + h:N_HEADS + h + 1, :])
    m_prev = each(lambda b, h: m0_ref[b][h:h + 1, 0:1])
    n = range(len(items))

    qk = [_dot_nt(q[j], k[j]) for j in n]
    qc = [_dot(q[j].astype(BF16), c_old[j].astype(BF16)) for j in n]
    qn = [jnp.sum(q[j] * n_old[j], axis=1, keepdims=True) for j in n]
    dm = [jnp.where(causal, b_col[j] - b_row[j] + i_row[j], -jnp.inf) for j in n]
    a_col = [b_col[j] + m_prev[j] for j in n]
    m_t = [jnp.maximum(a_col[j], jnp.max(dm[j], axis=1, keepdims=True)) for j in n]
    s = [qk[j] * jnp.exp(dm[j] - m_t[j]) for j in n]
    w_inter = [jnp.exp(a_col[j] - m_t[j]) for j in n]
    num = [_dot(s[j], v[j]) + w_inter[j] * qc[j] for j in n]
    den = [jnp.sum(s[j], axis=1, keepdims=True) + w_inter[j] * qn[j] for j in n]
    hh = [num[j] / jnp.maximum(jnp.abs(den[j]), jnp.exp(-m_t[j])) for j in n]
    ms = [jnp.mean(hh[j] * hh[j], axis=-1, keepdims=True) for j in n]
    for j, (b, h) in enumerate(items):
        h_ref[b, :, sl(h)] = hh[j] * lax.rsqrt(ms[j] + NORM_EPS) * gm_ref[:, sl(h)] * o_ref[b][:, sl(h)]

    m_new = [m_t[j][L - 1:L, :] for j in n]
    b_last = [b_col[j][L - 1:L, :] for j in n]
    decay = [jnp.exp(b_last[j] + m_prev[j] - m_new[j]) for j in n]
    ws_row = [jnp.exp(b_last[j] - b_row[j] + i_row[j] - m_new[j]) for j in n]
    ws_col = [jnp.exp(b_last[j] - b_col[j] + i_col[j] - m_new[j]) for j in n]
    kv = [_dot(kT_ref[b][sl(h), :] * ws_row[j], v[j]) for j, (b, h) in enumerate(items)]
    for j, (b, h) in enumerate(items):
        c_out_ref[b, h] = decay[j] * c_old[j] + kv[j]
        n_out_ref[b, h:h + 1, :] = decay[j] * n_old[j] + jnp.sum(k[j] * ws_col[j], axis=0, keepdims=True)
        m_out_ref[b, h:h + 1, :] = jnp.broadcast_to(m_new[j], (1, HEAD_DIM))


def _mlstm_sample(q, k, kT, v, o, gcol, grow, c0, n0, m0, g_mhead):
    db, L, _ = q.shape
    sb = SEQS_PER_STEP if db % SEQS_PER_STEP == 0 else 1
    seq = lambda *shape: pl.BlockSpec((sb,) + shape, lambda b: (b,) + (0,) * len(shape))
    m0_rep = jnp.broadcast_to(m0[:, :, None], (db, N_HEADS, HEAD_DIM))
    nbytes = sb * (2 * 2 * _nbytes((N_HEADS, HEAD_DIM, HEAD_DIM), F32) + 16 * _nbytes((8, WIDTH), F32))
    outs = pl.pallas_call(
        _mlstm_sample_kernel,
        grid=(db // sb,),
        in_specs=[seq(L, WIDTH), seq(L, WIDTH), seq(WIDTH, L), seq(L, WIDTH), seq(L, WIDTH),
                  seq(L, GATE_LANES), seq(GATE_ROWS, L),
                  seq(N_HEADS, HEAD_DIM, HEAD_DIM), seq(N_HEADS, HEAD_DIM), seq(N_HEADS, HEAD_DIM),
                  _const_spec((1, WIDTH))],
        out_specs=[seq(L, WIDTH), seq(N_HEADS, HEAD_DIM, HEAD_DIM), seq(N_HEADS, HEAD_DIM),
                   seq(N_HEADS, HEAD_DIM)],
        out_shape=[jax.ShapeDtypeStruct((db, L, WIDTH), F32),
                   jax.ShapeDtypeStruct((db, N_HEADS, HEAD_DIM, HEAD_DIM), F32),
                   jax.ShapeDtypeStruct((db, N_HEADS, HEAD_DIM), F32),
                   jax.ShapeDtypeStruct((db, N_HEADS, HEAD_DIM), F32)],
        compiler_params=_params(("parallel",), nbytes),
        name="mlstm_sample",
    )(q, k, kT, v, o, gcol, grow, c0, n0, m0_rep, g_mhead.reshape(1, WIDTH))
    hm, c_new, n_new, m_new = outs
    return hm, c_new, n_new, m_new[:, :, 0]


def _rel_bucket(dist):
    max_exact = REL_BUCKETS // 2
    n = jnp.maximum(dist, 0)
    nf = jnp.maximum(n, 1).astype(F32)
    large = max_exact + (jnp.log(nf / max_exact) / math.log(REL_MAX_DIST / max_exact)
                         * (REL_BUCKETS - max_exact)).astype(jnp.int32)
    large = jnp.minimum(large, REL_BUCKETS - 1)
    return jnp.where(n < max_exact, n, large)


def _bias_of_bucket(bucket, rb_ref, h):
    out = jnp.zeros(bucket.shape, F32)
    for b in range(REL_BUCKETS):
        out = jnp.where(bucket == b, rb_ref[b * N_HEADS + h], out)
    return out


def _rel_bias_from_dist(dist, rb_ref, h):
    return _bias_of_bucket(_rel_bucket(dist), rb_ref, h)


def _prompt_bias_kernel(rb_ref, o_ref):
    delta = pl.program_id(0)
    s = lax.broadcasted_iota(jnp.int32, (MOBA_BLOCK, MOBA_BLOCK), 0)
    t = lax.broadcasted_iota(jnp.int32, (MOBA_BLOCK, MOBA_BLOCK), 1)
    dist = delta * MOBA_BLOCK + t - s
    bucket = _rel_bucket(dist)
    for h in range(N_HEADS):
        bias = _bias_of_bucket(bucket, rb_ref, h) - rb_ref[(REL_BUCKETS - 1) * N_HEADS + h]
        o_ref[0, h] = jnp.where(dist >= 0, bias * LOG2E, NEG_INF)


def _near_blocks():
    return -(-(REL_SAT + MOBA_BLOCK - 1) // MOBA_BLOCK)


def _prompt_bias(rel_bias):
    nd = _near_blocks() + 1
    return pl.pallas_call(
        _prompt_bias_kernel,
        grid=(nd,),
        in_specs=[pl.BlockSpec(memory_space=pltpu.SMEM)],
        out_specs=pl.BlockSpec((1, N_HEADS, MOBA_BLOCK, MOBA_BLOCK), lambda d: (d, 0, 0, 0)),
        out_shape=jax.ShapeDtypeStruct((nd, N_HEADS, MOBA_BLOCK, MOBA_BLOCK), F32),
        compiler_params=_params(("parallel",), 4 * N_HEADS * _nbytes((MOBA_BLOCK, MOBA_BLOCK), F32)),
        name="prompt_bias",
    )(rel_bias.reshape(-1))


def _top_mask(sc, valid, ksel, axis):
    n = sc.shape[axis]
    iota = lax.broadcasted_iota(jnp.int32, sc.shape, axis)
    cur = jnp.where(valid, sc, NEG_INF)
    sel = jnp.zeros(sc.shape, F32)
    for _ in range(ksel):
        mx = jnp.max(cur, axis=axis, keepdims=True)
        idx = jnp.min(jnp.where(cur == mx, iota, n), axis=axis, keepdims=True)
        hit = iota == idx
        sel = jnp.where(hit, 1.0, sel)
        cur = jnp.where(hit, -jnp.inf, cur)
    return jnp.where((sel > 0.0) & valid, 0.0, NEG_INF)


def _mask_rows(nb):
    return -(-nb // 16) * 16


def _prompt_select_kernel(ksel, kmean_ref, qT_ref, o_ref):
    tq = qT_ref.shape[2]
    nb = kmean_ref.shape[0]
    nbp = o_ref.shape[2] - HEAD_DIM
    q0 = pl.program_id(0) * tq
    own = (q0 + lax.broadcasted_iota(jnp.int32, (nb, tq), 1)) // MOBA_BLOCK
    blk = lax.broadcasted_iota(jnp.int32, (nb, tq), 0)
    for h in range(N_HEADS):
        sl = slice(h * HEAD_DIM, (h + 1) * HEAD_DIM)
        q = qT_ref[0, sl, :]
        sc = jnp.dot(kmean_ref[:, sl], q, precision=HIGHEST, preferred_element_type=F32)
        mask = jnp.where(blk == own, 0.0, _top_mask(sc, blk < own, ksel, 0))
        o_ref[0, h, 0:HEAD_DIM, :] = (q * (HEAD_DIM ** -0.5 * LOG2E)).astype(BF16)
        o_ref[0, h, HEAD_DIM:HEAD_DIM + nb, :] = mask.astype(BF16)
        if nbp > nb:
            o_ref[0, h, HEAD_DIM + nb:, :] = jnp.zeros((nbp - nb, tq), BF16)


def _prompt_select(kmean, qT):
    nb = kmean.shape[0]
    nt, _, tq = qT.shape
    ksel = min(MOBA_TOPK, nb - 1)
    rows = HEAD_DIM + _mask_rows(nb)
    return pl.pallas_call(
        functools.partial(_prompt_select_kernel, ksel),
        grid=(nt,),
        in_specs=[_const_spec((nb, WIDTH)), pl.BlockSpec((1, WIDTH, tq), lambda i: (i, 0, 0))],
        out_specs=pl.BlockSpec((1, N_HEADS, rows, tq), lambda i: (i, 0, 0, 0)),
        out_shape=jax.ShapeDtypeStruct((nt, N_HEADS, rows, tq), BF16),
        compiler_params=_params(("parallel",), 4 * _nbytes((WIDTH, tq), F32)),
        name="prompt_select",
    )(kmean, qT)


def _moba_prompt_kernel(n_bias, qa_ref, k_ref, vT_ref, bias_ref, o_ref):
    i = pl.program_id(1)
    g_blocks = KV_GROUP
    rows = g_blocks * MOBA_BLOCK
    qa = qa_ref[0, 0]
    n_groups = i // g_blocks + 1
    far_groups = jnp.maximum(i - (n_bias - 2), 0) // g_blocks

    def logits(g, near):
        start = pl.multiple_of(g * rows, rows)
        sT = _dot(k_ref[0, pl.ds(start, rows), :], qa)
        if near:
            sT = jnp.concatenate(
                [sT[b * MOBA_BLOCK:(b + 1) * MOBA_BLOCK]
                 + bias_ref[jnp.clip(i - (g * g_blocks + b), 0, n_bias - 1), 0] for b in range(g_blocks)], axis=0)
        return sT, jnp.max(sT, axis=0, keepdims=True)

    def fold(g, sT, mx, state):
        m, l, acc = state
        m_new = jnp.maximum(m, mx)
        alpha = jnp.exp2(m - m_new)
        l, acc = alpha * l, alpha * acc
        for b in range(g_blocks):
            p = jnp.exp2(sT[b * MOBA_BLOCK:(b + 1) * MOBA_BLOCK] - m_new)
            l = l + jnp.sum(p, axis=0, keepdims=True)
            acc = acc + _dot(vT_ref[g * g_blocks + b], p.astype(BF16))
        return m_new, l, acc

    def stage(next_near):
        def body(g, carry):
            m, l, acc, s_cur, mx_cur = carry
            nxt = logits(g + 1, next_near)
            return fold(g, s_cur, mx_cur, (m, l, acc)) + nxt
        return body

    tq = qa.shape[1]
    carry = (jnp.full((1, tq), NEG_INF, F32), jnp.zeros((1, tq), F32), jnp.zeros((HEAD_DIM, tq), F32)) + logits(0, True)
    carry = lax.fori_loop(0, far_groups - 1, stage(False), carry)
    carry = lax.fori_loop(jnp.maximum(far_groups - 1, 0), n_groups - 1, stage(True), carry)
    m, l, acc = fold(n_groups - 1, carry[3], carry[4], carry[:3])
    o_ref[...] = (acc / l).T


def _moba_prompt(qa, k_aug, vT_bf, bias):
    nb, _, ka, _ = qa.shape
    t = k_aug.shape[1]
    n_bias = bias.shape[0]
    assert t == nb * MOBA_BLOCK and nb % KV_GROUP == 0 and vT_bf.shape == (nb, WIDTH, MOBA_BLOCK)
    nbytes = (2 * (_nbytes((t, -(-ka // 128) * 128), BF16) + _nbytes((t, HEAD_DIM), BF16)
                   + _nbytes((n_bias, MOBA_BLOCK, MOBA_BLOCK), F32))
              + 12 * KV_GROUP * _nbytes((MOBA_BLOCK, MOBA_BLOCK), F32))
    return pl.pallas_call(
        functools.partial(_moba_prompt_kernel, n_bias),
        grid=(N_HEADS, nb),
        in_specs=[pl.BlockSpec((1, 1, ka, MOBA_BLOCK), lambda h, i: (i, h, 0, 0)),
                  pl.BlockSpec((1, t, ka), lambda h, i: (h, 0, 0)),
                  pl.BlockSpec((nb, HEAD_DIM, MOBA_BLOCK), lambda h, i: (0, h, 0)),
                  pl.BlockSpec((n_bias, 1, MOBA_BLOCK, MOBA_BLOCK), lambda h, i: (0, h, 0, 0))],
        out_specs=pl.BlockSpec((MOBA_BLOCK, HEAD_DIM), lambda h, i: (i, h)),
        out_shape=jax.ShapeDtypeStruct((nb * MOBA_BLOCK, WIDTH), F32),
        compiler_params=_params(("arbitrary", "arbitrary"), nbytes),
        name="moba_prompt",
    )(qa, k_aug, vT_bf, bias)


def _sample_bias_kernel(past_len, t_new, rb_ref, past_ref, page_ref, new_ref):
    page = pl.program_id(0)

    def table(n_cols, first_kpos, masked):
        shape = (N_HEADS * t_new, n_cols)
        r = lax.broadcasted_iota(jnp.int32, shape, 0)
        c = lax.broadcasted_iota(jnp.int32, shape, 1)
        tpos = past_len + r % t_new
        kpos = first_kpos + c // N_HEADS
        bucket = _rel_bucket(tpos - kpos)
        ok = (kpos >= (tpos // MOBA_BLOCK) * MOBA_BLOCK) & (kpos <= tpos) if masked else None
        out = jnp.full(shape, NEG_INF, F32)
        for h in range(N_HEADS):
            same = (r // t_new == h) & (c % N_HEADS == h)
            out = jnp.where(same if ok is None else same & ok, _bias_of_bucket(bucket, rb_ref, h) * LOG2E, out)
        return out

    for u in range(past_ref.shape[0]):
        past_ref[u] = table(PAGE_ROWS, (page * past_ref.shape[0] + u) * PAGE_SIZE, False)

    @pl.when(page == 0)
    def _():
        page_ref[...] = table(PAGE_ROWS, past_len - PAGE_SIZE, True)
        new_ref[...] = table(new_ref.shape[1], past_len, True)


def _sample_bias(rel_bias, past_len, t_new):
    rows = N_HEADS * t_new
    n_pages = past_len // PAGE_SIZE
    new_cols = t_new * N_HEADS
    pg = math.gcd(n_pages, 8)
    return pl.pallas_call(
        functools.partial(_sample_bias_kernel, past_len, t_new),
        grid=(n_pages // pg,),
        in_specs=[pl.BlockSpec(memory_space=pltpu.SMEM)],
        out_specs=[pl.BlockSpec((pg, rows, PAGE_ROWS), lambda p: (p, 0, 0)),
                   pl.BlockSpec((rows, PAGE_ROWS), lambda p: (0, 0)),
                   pl.BlockSpec((rows, new_cols), lambda p: (0, 0))],
        out_shape=[jax.ShapeDtypeStruct((n_pages, rows, PAGE_ROWS), F32),
                   jax.ShapeDtypeStruct((rows, PAGE_ROWS), F32),
                   jax.ShapeDtypeStruct((rows, new_cols), F32)],
        compiler_params=_params(("arbitrary",), 16 * _nbytes((rows, PAGE_ROWS), F32)),
        name="sample_bias",
    )(rel_bias.reshape(-1))


def _page_specs(pps, n_pages):
    def spec(u):
        return pl.BlockSpec((1, PAGE_ROWS, HEAD_DIM), lambda b, c, pt: (pt[b * n_pages + c * pps + u], 0, 0))
    return [spec(u) for u in range(pps)]


def _softmax_partial(logit_tiles, value_tiles):
    m = jnp.max(logit_tiles[0], axis=1, keepdims=True)
    for lg in logit_tiles[1:]:
        m = jnp.maximum(m, jnp.max(lg, axis=1, keepdims=True))
    l, o = 0.0, 0.0
    for lg, v in zip(logit_tiles, value_tiles):
        p = jnp.exp2(lg - m)
        l = l + jnp.sum(p, axis=1, keepdims=True)
        o = o + _dot(p.astype(BF16), v.astype(BF16))
    return m, l, o


def _moba_sample_kernel(pps, n_blocks, t_new, pt_ref, q_ref, knew_ref, vnew_ref,
                        bpast_ref, bpage_ref, bnew_ref, *refs):
    k_refs, v_refs = refs[:pps], refs[pps:2 * pps]
    o_ref = refs[2 * pps]
    sc_s, m_s, l_s, o_s = refs[2 * pps + 1:]
    del pt_ref
    c = pl.program_id(1)
    nch = pl.num_programs(1)
    ppb = MOBA_BLOCK // PAGE_SIZE
    bps = pps // ppb
    rows = N_HEADS * t_new
    fold = 8 // N_HEADS
    ksel = min(MOBA_TOPK, n_blocks)
    q = q_ref[0]
    qb = (q * (HEAD_DIM ** -0.5 * LOG2E)).astype(BF16)
    lanes = lambda a: jnp.broadcast_to(a, (rows, HEAD_DIM))

    pages = [k_refs[u][0] for u in range(pps)]
    tiles = [_dot_nt(qb, pages[u].astype(BF16)) + bpast_ref[c * pps + u] for u in range(pps)]
    for b in range(bps):
        n = c * bps + b
        ksum = sum(jnp.sum(pages[b * ppb + u].reshape(PAGE_ROWS // 8, 8, HEAD_DIM), axis=0)
                   for u in range(ppb))
        kmean = sum(ksum[f * N_HEADS:(f + 1) * N_HEADS] for f in range(fold)) / MOBA_BLOCK
        kmean_rows = jnp.concatenate(
            [jnp.broadcast_to(kmean[h:h + 1], (t_new, HEAD_DIM)) for h in range(N_HEADS)], axis=0)
        sc_s[n] = lanes(jnp.sum(q * kmean_rows, axis=1, keepdims=True))
        m, l, o = _softmax_partial(tiles[b * ppb:(b + 1) * ppb], [v_refs[b * ppb + u][0] for u in range(ppb)])
        m_s[n], l_s[n], o_s[n] = lanes(m), lanes(l), o

    @pl.when(c == nch - 1)
    def _():
        own_page = _dot_nt(qb, pages[pps - 1].astype(BF16)) + bpage_ref[...]
        m_pg, l_pg, o_pg = _softmax_partial([own_page], [v_refs[pps - 1][0]])
        new = _dot_nt(qb, knew_ref[0].astype(BF16)) + bnew_ref[...]
        m_nw, l_nw, o_nw = _softmax_partial([new], [vnew_ref[0]])

        lane = lax.broadcasted_iota(jnp.int32, (rows, HEAD_DIM), 1)
        r = lax.broadcasted_iota(jnp.int32, (rows, HEAD_DIM), 0)
        sc, mm, ll = jnp.zeros((rows, HEAD_DIM), F32), jnp.zeros((rows, HEAD_DIM), F32), jnp.zeros((rows, HEAD_DIM), F32)
        for n in range(n_blocks):
            at = lane == n
            sc, mm, ll = jnp.where(at, sc_s[n], sc), jnp.where(at, m_s[n], mm), jnp.where(at, l_s[n], ll)
        tpos = n_blocks * MOBA_BLOCK + r % t_new
        valid = (lane < tpos // MOBA_BLOCK) & (lane < n_blocks)
        mm = jnp.where(lane < n_blocks, mm + _top_mask(sc, valid, ksel, 1), NEG_INF)
        m_all = jnp.maximum(jnp.max(mm, axis=1, keepdims=True), jnp.maximum(m_pg, m_nw))
        w = jnp.exp2(mm - m_all)
        w_pg, w_nw = jnp.exp2(m_pg - m_all), jnp.exp2(m_nw - m_all)
        den = jnp.sum(w * ll, axis=1, keepdims=True) + w_pg * l_pg + w_nw * l_nw
        num = w_pg * o_pg + w_nw * o_nw
        for n in range(n_blocks):
            num = num + jnp.sum(jnp.where(lane == n, w, 0.0), axis=1, keepdims=True) * o_s[n]
        o_ref[0] = num / den


def _moba_sample(q_rows, k_new, v_new, cache_k, cache_v, page_table, bias_past, bias_page, bias_new):
    db, rows, _ = q_rows.shape
    t_new = rows // N_HEADS
    n_pages = page_table.shape[1]
    ppb = MOBA_BLOCK // PAGE_SIZE
    assert n_pages % ppb == 0 and 8 % N_HEADS == 0 and t_new % 8 == 0
    n_blocks = n_pages // ppb
    assert n_blocks <= HEAD_DIM, "per-block scalars are gathered into one lane tile"
    pps = min(PAGES_PER_STEP, n_pages)
    assert n_pages % pps == 0 and pps % ppb == 0
    seq = lambda *shape: pl.BlockSpec((1,) + shape, lambda b, c, pt: (b,) + (0,) * len(shape))
    slab = pltpu.VMEM((n_blocks, rows, HEAD_DIM), F32)
    nbytes = (4 * pps * _nbytes((PAGE_ROWS, HEAD_DIM), F32) + _nbytes(bias_past.shape, F32)
              + 4 * _nbytes((n_blocks, rows, HEAD_DIM), F32) + 64 * _nbytes((rows, PAGE_ROWS), F32))
    return pl.pallas_call(
        functools.partial(_moba_sample_kernel, pps, n_blocks, t_new),
        grid_spec=pltpu.PrefetchScalarGridSpec(
            num_scalar_prefetch=1,
            grid=(db, n_pages // pps),
            in_specs=[seq(rows, HEAD_DIM), seq(rows, HEAD_DIM), seq(rows, HEAD_DIM),
                      _const_spec(bias_past.shape), _const_spec(bias_page.shape), _const_spec(bias_new.shape)]
                     + _page_specs(pps, n_pages) + _page_specs(pps, n_pages),
            out_specs=seq(rows, HEAD_DIM),
            scratch_shapes=[slab, slab, slab, slab]),
        out_shape=jax.ShapeDtypeStruct((db, rows, HEAD_DIM), F32),
        compiler_params=_params(("parallel", "arbitrary"), nbytes),
        name="moba_sample",
    )(page_table.reshape(-1), q_rows, k_new, v_new, bias_past, bias_page, bias_new,
      *([cache_k] * pps), *([cache_v] * pps))


def _out_ffn_kernel(ff_chunk, x_ref, hm_ref, ha_ref, sgm_ref, sga_ref, gate1_ref, shift2_ref, scale2_ref,
                    gate2_ref, g_ref, wbm_ref, wba_ref, wout_ref, wup_ref, wdn_ref, y_ref):
    bm = _dot(hm_ref[...].astype(BF16), wbm_ref[...])
    ba = _dot(ha_ref[...].astype(BF16), wba_ref[...])
    mix = sgm_ref[...] * bm + sga_ref[...] * ba
    x1 = x_ref[...] + gate1_ref[...] * _dot(mix.astype(BF16), wout_ref[...])
    y = x1 * lax.rsqrt(jnp.mean(x1 * x1, axis=-1, keepdims=True) + NORM_EPS) * g_ref[...]
    h2 = (y * (1.0 + scale2_ref[...]) + shift2_ref[...]).astype(BF16)
    d_ff = wup_ref.shape[1]
    acc = jnp.zeros(x1.shape, F32)
    for c in range(d_ff // ff_chunk):
        u = jnp.maximum(_dot(h2, wup_ref[:, c * ff_chunk:(c + 1) * ff_chunk]), 0.0)
        acc = acc + _dot((u * u).astype(BF16), wdn_ref[c * ff_chunk:(c + 1) * ff_chunk, :])
    y_ref[...] = x1 + gate2_ref[...] * acc


def _out_ffn(x, hm, ha, sgm, sga, gate1, shift2, scale2, gate2, g_norm, wbm, wba, wout, wup, wdn, tm):
    t, d = x.shape
    per_tok = gate1.shape[0] != 1
    tok = lambda n: pl.BlockSpec((tm, n), lambda i: (i, 0))
    mod = tok(d) if per_tok else pl.BlockSpec((1, d), lambda i: (0, 0))
    weights = (wbm, wba, wout, wup, wdn)
    nbytes = (sum(_nbytes(w.shape, w.dtype) for w in weights)
              + 2 * _nbytes((tm, d), F32) * (5 + (4 if per_tok else 0)) + 8 * _nbytes((tm, d), F32))
    return pl.pallas_call(
        functools.partial(_out_ffn_kernel, 1024),
        grid=(t // tm,),
        in_specs=[tok(d), tok(WIDTH), tok(WIDTH), tok(d), tok(d), mod, mod, mod, mod, _const_spec((1, d))]
                 + [_const_spec(w.shape) for w in weights],
        out_specs=tok(d),
        out_shape=jax.ShapeDtypeStruct((t, d), F32),
        compiler_params=_params(("parallel",), nbytes),
        name="out_ffn",
    )(x, hm, ha, sgm, sga, gate1, shift2, scale2, gate2, g_norm.reshape(1, d), *weights)


def _layer(x_prompt, x_sample, cache_k, cache_v, state_c, state_n, state_m, page_table, c_prompt, c_sample,
           rel_bias, w_ada, b_ada, g_norm_mix, w_in, b_igate, b_fgate, g_mhead, g_qnorm, g_knorm,
           w_branch_m, w_branch_a, w_out, g_norm_ffn, w_ff_up, w_ff_down):
    bsz, seq, d = x_prompt.shape
    db, t_new, _ = x_sample.shape
    assert bsz == 1, "the prompt path handles one sequence"
    n_pages = page_table.shape[1]
    past_len = n_pages * PAGE_SIZE
    k_scale = HEAD_DIM ** -0.5

    c_all = jnp.concatenate([c_prompt, c_sample], axis=0)
    pad = (-c_all.shape[0]) % 8
    mod = _ada(jnp.pad(c_all, ((0, pad), (0, 0))), w_ada, b_ada)
    mod_p = jnp.split(mod[0:1], N_COND, axis=-1)
    mod_s = jnp.split(jnp.repeat(mod[1:1 + db], t_new, axis=0), N_COND, axis=-1)

    cols = _split_w_in(w_in)
    gate_col, gate_row = _gate_groups(cols, b_igate, b_fgate)
    gq_row, gk_row = g_qnorm.reshape(1, WIDTH), g_knorm.reshape(1, WIDTH)
    gq_col = g_qnorm.reshape(WIDTH, 1)
    wbm, wba, wout = w_branch_m.astype(BF16), w_branch_a.astype(BF16), w_out.astype(BF16)
    wup, wdn = w_ff_up.astype(BF16), w_ff_down.astype(BF16)

    plan_p = [
        _group(cols["mq"], True), _group(cols["mk"], False, "scale", const=k_scale), _group(cols["mv"], True),
        _group(cols["mo"], True, "sigmoid"), gate_col, gate_row,
        _group(cols["aq"], True, "norm_col", auxv=gq_col),
        _group(cols["ak"], False, "norm_row", auxv=gk_row, outs=("heads", BF16, "mean")),
        _group(cols["av"], False, outs=("heads",)), _group(cols["av"], True, outs=(BF16,)),
        _group(cols["gm"], False, "sigmoid"), _group(cols["ga"], False, "sigmoid"),
    ]
    xp = x_prompt.reshape(seq, d)
    (mqT, mk, mvT, soT, gcol, grow, aqT, ak, ak_bf, kmean, av, avT_bf, sgm, sga) = _in_proj(
        xp, mod_p[0], mod_p[1], g_norm_mix, plan_p, TOKEN_TILE)
    hm, c_p, n_p, m_p = _mlstm_prompt(mqT, mk, mvT, soT, gcol, grow, g_mhead)
    nb = seq // MOBA_BLOCK
    qa = _prompt_select(kmean.reshape(nb, WIDTH), aqT)
    block_id = jnp.arange(seq, dtype=jnp.int32)[:, None] // MOBA_BLOCK
    one_hot = (block_id == jnp.arange(_mask_rows(nb), dtype=jnp.int32)[None, :]).astype(BF16)
    k_aug = jnp.concatenate([ak_bf.reshape(seq, N_HEADS, HEAD_DIM).transpose(1, 0, 2),
                             jnp.broadcast_to(one_hot, (N_HEADS,) + one_hot.shape)], axis=-1)
    ha = _moba_prompt(qa, k_aug, avT_bf, _prompt_bias(rel_bias))
    y_p = _out_ffn(xp, hm, ha, sgm, sga, mod_p[2], mod_p[3], mod_p[4], mod_p[5], g_norm_ffn,
                   wbm, wba, wout, wup, wdn, TOKEN_TILE)

    plan_s = [
        _group(cols["mq"], False), _group(cols["mk"], False, "scale", const=k_scale), _group(cols["mv"], False),
        _group(cols["mo"], False, "sigmoid"), gate_col, gate_row,
        _group(cols["aq"], False, "norm_row", auxv=gq_row),
        _group(cols["ak"], False, "norm_row", auxv=gk_row, outs=("heads",)),
        _group(cols["av"], False, outs=("heads",)),
        _group(cols["gm"], False, "sigmoid"), _group(cols["ga"], False, "sigmoid"),
    ]
    ts = db * t_new
    xs = x_sample.reshape(ts, d)
    (mq_s, mk_s, mv_s, so_s, gcol_s, grow_s, aq_s, ak_s, av_s, sgm_s, sga_s) = _in_proj(
        xs, mod_s[0], mod_s[1], g_norm_mix, plan_s, min(TOKEN_TILE, ts))
    per_seq = lambda a: a.reshape(db, t_new, a.shape[-1])
    mk_seq = per_seq(mk_s)
    grow_seq = jnp.transpose(grow_s, (1, 0, 2)).reshape(GATE_ROWS, db, t_new).transpose(1, 0, 2)
    hm_s, c_s, n_s, m_s = _mlstm_sample(per_seq(mq_s), mk_seq, jnp.swapaxes(mk_seq, 1, 2), per_seq(mv_s),
                                        per_seq(so_s), per_seq(gcol_s), grow_seq,
                                        state_c, state_n, state_m, g_mhead)
    rows = N_HEADS * t_new
    ck = cache_k.reshape(cache_k.shape[0], PAGE_ROWS, HEAD_DIM)
    cv = cache_v.reshape(cache_v.shape[0], PAGE_ROWS, HEAD_DIM)
    head_major = lambda a: a.reshape(db, t_new, N_HEADS, HEAD_DIM).transpose(0, 2, 1, 3).reshape(db, rows, HEAD_DIM)
    b_past, b_page, b_new = _sample_bias(rel_bias, past_len, t_new)
    ha_rows = _moba_sample(head_major(aq_s), ak_s.reshape(db, rows, HEAD_DIM), av_s.reshape(db, rows, HEAD_DIM),
                           ck, cv, page_table, b_past, b_page, b_new)
    ha_s = ha_rows.reshape(db, N_HEADS, t_new, HEAD_DIM).transpose(0, 2, 1, 3)
    y_s = _out_ffn(xs, hm_s.reshape(ts, WIDTH), ha_s.reshape(ts, WIDTH), sgm_s, sga_s,
                   mod_s[2], mod_s[3], mod_s[4], mod_s[5], g_norm_ffn, wbm, wba, wout, wup, wdn,
                   min(TOKEN_TILE, ts))

    heads = lambda a, lead: a.reshape(lead + (N_HEADS, HEAD_DIM))
    return (y_p.reshape(bsz, seq, d), y_s.reshape(db, t_new, d),
            heads(ak, (bsz, seq)), heads(av, (bsz, seq)), c_p[None], n_p[None], m_p[None],
            heads(ak_s, (db, t_new)), heads(av_s, (db, t_new)), c_s, n_s, m_s)


def kernel(x_prompt, x_sample, cache_k, cache_v, state_C, state_n, state_m, page_table, c_prompt, c_sample,
           rel_bias, w_ada, b_ada, g_norm_mix, w_in, b_igate, b_fgate, g_mhead, g_qnorm, g_knorm,
           w_branch_m, w_branch_a, w_out, g_norm_ffn, w_ff_up, w_ff_down):
    depth = w_in.shape[0]
    assert depth == 1, "state outputs are stacked per layer; this kernel implements the single-layer trunk"
    outs = _layer(x_prompt, x_sample, cache_k[0], cache_v[0], state_C[0], state_n[0], state_m[0], page_table,
                  c_prompt, c_sample, rel_bias, w_ada[0], b_ada[0], g_norm_mix[0], w_in[0], b_igate[0],
                  b_fgate[0], g_mhead[0], g_qnorm[0], g_knorm[0], w_branch_m[0], w_branch_a[0], w_out[0],
                  g_norm_ffn[0], w_ff_up[0], w_ff_down[0])
    yp, ys, kp, vp, cp, np_, mp, ks, vs, cs, ns, ms = outs
    stack = lambda a: a[None]
    return (yp, ys, stack(kp), stack(vp), stack(cp), stack(np_), stack(mp),
            stack(ks), stack(vs), stack(cs), stack(ns), stack(ms))
```

```python
import functools
import math

import jax
import jax.numpy as jnp
from jax import lax
from jax.experimental import pallas as pl
from jax.experimental.pallas import tpu as pltpu

F32 = jnp.float32
BF16 = jnp.bfloat16
HIGHEST = lax.Precision.HIGHEST

N_HEADS = 4
HEAD_DIM = 128
WIDTH = N_HEADS * HEAD_DIM
MOBA_BLOCK = 256
MOBA_TOPK = 3
PAGE_SIZE = 128
REL_BUCKETS = 32
REL_MAX_DIST = 4096
N_COND = 6
NORM_EPS = 1e-6
NEG_INF = -1e30
LOG2E = math.log2(math.e)
GATE_LANES = 128
GATE_ROWS = 16

V7X_VMEM_BYTES = 64 * 1024 * 1024
VMEM_HEADROOM_BYTES = 8 * 1024 * 1024
TOKEN_TILE = 256
assert TOKEN_TILE == MOBA_BLOCK, "feature-major projection tiles double as MoBA key/query blocks"
PAGES_PER_STEP = 32
KV_GROUP = 4
SEQS_PER_STEP = 2
PAGE_ROWS = PAGE_SIZE * N_HEADS

REL_SAT = 3072


def _vmem_limit(nbytes):
    return int(min(V7X_VMEM_BYTES - VMEM_HEADROOM_BYTES, nbytes + VMEM_HEADROOM_BYTES))


def _nbytes(shape, dtype):
    return math.prod(shape) * jnp.dtype(dtype).itemsize


def _params(sem, nbytes):
    return pltpu.CompilerParams(dimension_semantics=sem, vmem_limit_bytes=_vmem_limit(nbytes))


def _const_spec(shape):
    nd = len(shape)
    return pl.BlockSpec(shape, lambda *_: (0,) * nd, pipeline_mode=pl.Buffered(1))


def _dot(a, b):
    return jnp.dot(a, b, preferred_element_type=F32)


def _dot_nt(a, b):
    return lax.dot_general(a, b, (((1,), (1,)), ((), ())), preferred_element_type=F32)


def _log_sigmoid(x):
    return -(jnp.maximum(-x, 0.0) + jnp.log1p(jnp.exp(-jnp.abs(x))))


def _ada_kernel(c_ref, w_ref, b_ref, o_ref):
    c = c_ref[...]
    s = (c * jax.nn.sigmoid(c)).astype(BF16)
    o_ref[...] = _dot(s, w_ref[...].astype(BF16)) + b_ref[...]


def _ada(c_all, w_ada, b_ada):
    rows, d = c_all.shape
    n_out = w_ada.shape[1]
    tn = d
    nbytes = 2 * (_nbytes((rows, d), F32) + _nbytes((d, tn), F32) + _nbytes((rows, tn), F32))
    return pl.pallas_call(
        _ada_kernel,
        grid=(n_out // tn,),
        in_specs=[pl.BlockSpec((rows, d), lambda j: (0, 0)),
                  pl.BlockSpec((d, tn), lambda j: (0, j)),
                  pl.BlockSpec((1, tn), lambda j: (0, j))],
        out_specs=pl.BlockSpec((rows, tn), lambda j: (0, j)),
        out_shape=jax.ShapeDtypeStruct((rows, n_out), F32),
        compiler_params=_params(("arbitrary",), nbytes),
        name="ada",
    )(c_all, w_ada, b_ada.reshape(1, n_out))


def _head_norm_rows(z, g):
    outs = []
    for h in range(N_HEADS):
        blk = z[:, h * HEAD_DIM:(h + 1) * HEAD_DIM]
        ms = jnp.mean(blk * blk, axis=-1, keepdims=True)
        outs.append(blk * lax.rsqrt(ms + NORM_EPS) * g[:, h * HEAD_DIM:(h + 1) * HEAD_DIM])
    return jnp.concatenate(outs, axis=-1)


def _head_norm_cols(z, g):
    outs = []
    for h in range(N_HEADS):
        blk = z[h * HEAD_DIM:(h + 1) * HEAD_DIM, :]
        ms = jnp.mean(blk * blk, axis=0, keepdims=True)
        outs.append(blk * lax.rsqrt(ms + NORM_EPS) * g[h * HEAD_DIM:(h + 1) * HEAD_DIM, :])
    return jnp.concatenate(outs, axis=0)


def _in_proj_kernel(plan, x_ref, shift_ref, scale_ref, g_ref, *refs):
    n_groups = len(plan)
    w_refs = refs[:n_groups]
    n_aux = sum(1 for p in plan if p["aux"])
    aux_refs = refs[n_groups:n_groups + n_aux]
    out_refs = refs[n_groups + n_aux:]

    x = x_ref[...]
    y = x * lax.rsqrt(jnp.mean(x * x, axis=-1, keepdims=True) + NORM_EPS) * g_ref[...]
    hb = (y * (1.0 + scale_ref[...]) + shift_ref[...]).astype(BF16)

    ai = 0
    oi = 0
    for p, w_ref in zip(plan, w_refs):
        z = _dot_nt(w_ref[...], hb) if p["trans"] else _dot(hb, w_ref[...])
        aux = None
        if p["aux"]:
            aux = aux_refs[ai][...]
            ai += 1
        epi = p["epi"]
        if epi == "scale":
            z = z * p["const"]
        elif epi == "sigmoid":
            z = jax.nn.sigmoid(z)
        elif epi == "norm_row":
            z = _head_norm_rows(z, aux)
        elif epi == "norm_col":
            z = _head_norm_cols(z, aux)
        elif epi == "gate_col":
            pre = z + aux
            lane = lax.broadcasted_iota(jnp.int32, pre.shape, 1)
            z = jnp.where(lane < N_HEADS, pre, _log_sigmoid(pre))
        elif epi == "gate_row":
            pre = z + aux
            row = lax.broadcasted_iota(jnp.int32, pre.shape, 0)
            z = jnp.where(row < N_HEADS, pre, _log_sigmoid(pre))
        for kind in p["outs"]:
            if kind == "heads":
                for h in range(N_HEADS):
                    out_refs[oi][:, h, :] = z[:, h * HEAD_DIM:(h + 1) * HEAD_DIM]
            elif kind == "mean":
                out_refs[oi][0] = jnp.mean(z, axis=0, keepdims=True)
            elif p["trans"]:
                out_refs[oi][0] = z.astype(kind)
            else:
                out_refs[oi][...] = z.astype(kind)
            oi += 1


def _in_proj(x, shift, scale, g_norm, plan, tm):
    t, d = x.shape
    per_tok = shift.shape[0] != 1
    mod_spec = (pl.BlockSpec((tm, d), lambda i: (i, 0)) if per_tok
                else pl.BlockSpec((1, d), lambda i: (0, 0)))
    in_specs = [pl.BlockSpec((tm, d), lambda i: (i, 0)), mod_spec, mod_spec, _const_spec((1, d))]
    args = [x, shift, scale, g_norm.reshape(1, d)]
    nbytes = 2 * _nbytes((tm, d), F32) * (3 if per_tok else 1)
    for p in plan:
        in_specs.append(_const_spec(p["w"].shape))
        args.append(p["w"])
        nbytes += _nbytes(p["w"].shape, p["w"].dtype)
    for p in plan:
        if p["aux"]:
            in_specs.append(_const_spec(p["auxv"].shape))
            args.append(p["auxv"])
    out_specs, out_shapes = [], []
    for p in plan:
        n = p["w"].shape[0] if p["trans"] else p["w"].shape[1]
        for kind in p["outs"]:
            if kind == "heads":
                assert not p["trans"] and n == WIDTH
                out_specs.append(pl.BlockSpec((tm, N_HEADS, HEAD_DIM), lambda i: (i, 0, 0)))
                out_shapes.append(jax.ShapeDtypeStruct((t, N_HEADS, HEAD_DIM), F32))
            elif kind == "mean":
                assert not p["trans"] and tm == MOBA_BLOCK
                out_specs.append(pl.BlockSpec((1, 1, n), lambda i: (i, 0, 0)))
                out_shapes.append(jax.ShapeDtypeStruct((t // tm, 1, n), F32))
            elif p["trans"]:
                out_specs.append(pl.BlockSpec((1, n, tm), lambda i: (i, 0, 0)))
                out_shapes.append(jax.ShapeDtypeStruct((t // tm, n, tm), kind))
            else:
                out_specs.append(pl.BlockSpec((tm, n), lambda i: (i, 0)))
                out_shapes.append(jax.ShapeDtypeStruct((t, n), kind))
            nbytes += 2 * _nbytes((tm, n), F32)
    plan_static = tuple({k: v for k, v in p.items() if k not in ("w", "auxv")} for p in plan)
    return pl.pallas_call(
        functools.partial(_in_proj_kernel, plan_static),
        grid=(t // tm,),
        in_specs=in_specs,
        out_specs=out_specs,
        out_shape=out_shapes,
        compiler_params=_params(("parallel",), nbytes + 4 * _nbytes((tm, 1024), F32)),
        name="in_proj",
    )(*args)


def _group(w, trans, epi="none", const=None, auxv=None, outs=(F32,)):
    w = (w.T if trans else w).astype(BF16)
    return dict(w=w, trans=trans, epi=epi, const=const, aux=auxv is not None, auxv=auxv, outs=tuple(outs))


def _split_w_in(w_in):
    sizes = (WIDTH, WIDTH, WIDTH, WIDTH, N_HEADS, N_HEADS, WIDTH, WIDTH, WIDTH, w_in.shape[0], w_in.shape[0])
    names = ("mq", "mk", "mv", "mo", "mi", "mf", "aq", "ak", "av", "gm", "ga")
    out, off = {}, 0
    for n, s in zip(names, sizes):
        out[n] = w_in[:, off:off + s]
        off += s
    assert off == w_in.shape[1]
    return out


def _gate_groups(cols, b_igate, b_fgate):
    w_gate = jnp.concatenate([cols["mi"], cols["mf"]], axis=1)
    bias = jnp.concatenate([b_igate, b_fgate]).astype(F32)
    pad_to = lambda a, n, axis: jnp.pad(a, [(0, n - a.shape[i]) if i == axis else (0, 0) for i in range(a.ndim)])
    return (_group(pad_to(w_gate, GATE_LANES, 1), False, "gate_col", auxv=pad_to(bias.reshape(1, -1), GATE_LANES, 1)),
            _group(pad_to(w_gate, GATE_ROWS, 1), True, "gate_row", auxv=pad_to(bias.reshape(-1, 1), GATE_ROWS, 0)))


def _mlstm_prompt_kernel(qT_ref, k_ref, vT_ref, oT_ref, gcol_ref, grow_ref, gm_ref,
                         h_ref, c_out_ref, n_out_ref, m_out_ref, c_s, n_s, m_s):
    step = pl.program_id(0)
    L = k_ref.shape[0]

    @pl.when(step == 0)
    def _():
        c_s[...] = jnp.zeros_like(c_s)
        n_s[...] = jnp.zeros_like(n_s)
        m_s[...] = jnp.zeros_like(m_s)

    r = lax.broadcasted_iota(jnp.int32, (L, L), 0)
    c = lax.broadcasted_iota(jnp.int32, (L, L), 1)
    tril = (c <= r).astype(F32)
    triu = (r <= c)
    gcol = gcol_ref[...]
    grow = grow_ref[0]
    bcol_all = jnp.dot(tril, gcol, precision=HIGHEST, preferred_element_type=F32)
    brow_all = jnp.dot(grow, triu.astype(F32), precision=HIGHEST, preferred_element_type=F32)

    hs = range(N_HEADS)
    sl = lambda h: slice(h * HEAD_DIM, (h + 1) * HEAD_DIM)
    qTb = [qT_ref[0, sl(h), :].astype(BF16) for h in hs]
    k = [k_ref[:, sl(h)] for h in hs]
    vTb = [vT_ref[0, sl(h), :].astype(BF16) for h in hs]
    m_prev = [m_s[h][0:1, 0:1] for h in hs]
    ct = [c_s[h] for h in hs]
    n_row = [n_s[h][0:1, :] for h in hs]
    b_row = [brow_all[N_HEADS + h:N_HEADS + h + 1, :] for h in hs]
    g_col = [gcol[:, h:h + 1] - bcol_all[:, N_HEADS + h:N_HEADS + h + 1] for h in hs]

    sT = [_dot(k[h].astype(BF16), qTb[h]) for h in hs]
    cq = [_dot(ct[h].astype(BF16), qTb[h]) for h in hs]
    nq = [_dot(jnp.broadcast_to(n_row[h], (8, HEAD_DIM)).astype(BF16), qTb[h])[0:1, :] for h in hs]
    dT = [jnp.where(triu, g_col[h] + b_row[h], -jnp.inf) for h in hs]
    a_row = [b_row[h] + m_prev[h] for h in hs]
    m_t = [jnp.maximum(a_row[h], jnp.max(dT[h], axis=0, keepdims=True)) for h in hs]
    pT = [sT[h] * jnp.exp(dT[h] - m_t[h]) for h in hs]
    w_inter = [jnp.exp(a_row[h] - m_t[h]) for h in hs]
    numT = [_dot(vTb[h], pT[h].astype(BF16)) + w_inter[h] * cq[h] for h in hs]
    den = [jnp.sum(pT[h], axis=0, keepdims=True) + w_inter[h] * nq[h] for h in hs]
    hT = [numT[h] / jnp.maximum(jnp.abs(den[h]), jnp.exp(-m_t[h])) for h in hs]
    ms = [jnp.mean(hT[h] * hT[h], axis=0, keepdims=True) for h in hs]
    for h in hs:
        yT = hT[h] * lax.rsqrt(ms[h] + NORM_EPS) * gm_ref[sl(h), :] * oT_ref[0, sl(h), :]
        h_ref[:, sl(h)] = yT.T

    m_new = [m_t[h][:, L - 1:L] for h in hs]
    b_last = [b_row[h][:, L - 1:L] for h in hs]
    decay = [jnp.exp(b_last[h] + m_prev[h] - m_new[h]) for h in hs]
    kw = [k[h] * jnp.exp(b_last[h] + g_col[h] - m_new[h]) for h in hs]
    c_new = [decay[h] * ct[h] + _dot(vTb[h], kw[h].astype(BF16)) for h in hs]
    for h in hs:
        c_s[h] = c_new[h]
        n_s[h] = jnp.broadcast_to(decay[h] * n_row[h] + jnp.sum(kw[h], axis=0, keepdims=True), (8, HEAD_DIM))
        m_s[h] = jnp.broadcast_to(m_new[h], (8, HEAD_DIM))

    @pl.when(step == pl.num_programs(0) - 1)
    def _():
        for h in range(N_HEADS):
            c_out_ref[h] = c_s[h].T
        n_out_ref[...] = n_s[...]
        m_out_ref[...] = m_s[...]


def _mlstm_prompt(qT, k, vT, oT, gcol, grow, g_mhead):
    t = k.shape[0]
    L = qT.shape[2]
    assert t % L == 0 and qT.shape[0] * L == t
    blk_bytes = 3 * _nbytes((WIDTH, L), F32) + 2 * _nbytes((L, WIDTH), F32) + _nbytes((L, GATE_LANES), F32)
    state = jax.ShapeDtypeStruct((N_HEADS, 8, HEAD_DIM), F32)
    outs = pl.pallas_call(
        _mlstm_prompt_kernel,
        grid=(t // L,),
        in_specs=[pl.BlockSpec((1, WIDTH, L), lambda i: (i, 0, 0)),
                  pl.BlockSpec((L, WIDTH), lambda i: (i, 0)),
                  pl.BlockSpec((1, WIDTH, L), lambda i: (i, 0, 0)),
                  pl.BlockSpec((1, WIDTH, L), lambda i: (i, 0, 0)),
                  pl.BlockSpec((L, GATE_LANES), lambda i: (i, 0)),
                  pl.BlockSpec((1, GATE_ROWS, L), lambda i: (i, 0, 0)),
                  _const_spec((WIDTH, 1))],
        out_specs=[pl.BlockSpec((L, WIDTH), lambda i: (i, 0)),
                   pl.BlockSpec((N_HEADS, HEAD_DIM, HEAD_DIM), lambda i: (0, 0, 0)),
                   pl.BlockSpec((N_HEADS, 8, HEAD_DIM), lambda i: (0, 0, 0)),
                   pl.BlockSpec((N_HEADS, 8, HEAD_DIM), lambda i: (0, 0, 0))],
        out_shape=[jax.ShapeDtypeStruct((t, WIDTH), F32),
                   jax.ShapeDtypeStruct((N_HEADS, HEAD_DIM, HEAD_DIM), F32), state, state],
        scratch_shapes=[pltpu.VMEM((N_HEADS, HEAD_DIM, HEAD_DIM), F32),
                        pltpu.VMEM((N_HEADS, 8, HEAD_DIM), F32),
                        pltpu.VMEM((N_HEADS, 8, HEAD_DIM), F32)],
        compiler_params=_params(("arbitrary",), 2 * blk_bytes + 16 * _nbytes((L, L), F32)),
        name="mlstm_prompt",
    )(qT, k, vT, oT, gcol, grow, g_mhead.reshape(WIDTH, 1))
    hm, c_fin, n_fin, m_fin = outs
    return hm, c_fin, n_fin[:, 0, :], m_fin[:, 0, 0]


def _mlstm_sample_kernel(q_ref, k_ref, kT_ref, v_ref, o_ref, gcol_ref, grow_ref, c0_ref, n0_ref, m0_ref,
                         gm_ref, h_ref, c_out_ref, n_out_ref, m_out_ref):
    L = q_ref.shape[1]
    r = lax.broadcasted_iota(jnp.int32, (L, L), 0)
    c = lax.broadcasted_iota(jnp.int32, (L, L), 1)
    causal = c <= r
    items = [(b, h) for b in range(q_ref.shape[0]) for h in range(N_HEADS)]
    sl = lambda h: slice(h * HEAD_DIM, (h + 1) * HEAD_DIM)
    each = lambda f: [f(b, h) for b, h in items]

    gcol = [gcol_ref[b] for b in range(q_ref.shape[0])]
    grow = [grow_ref[b] for b in range(q_ref.shape[0])]
    bcol = [jnp.dot(causal.astype(F32), g, precision=HIGHEST, preferred_element_type=F32) for g in gcol]
    brow = [jnp.dot(g, (r <= c).astype(F32), precision=HIGHEST, preferred_element_type=F32) for g in grow]

    q = each(lambda b, h: q_ref[b][:, sl(h)])
    k = each(lambda b, h: k_ref[b][:, sl(h)])
    v = each(lambda b, h: v_ref[b][:, sl(h)])
    c_old = each(lambda b, h: c0_ref[b, h])
    n_old = each(lambda b, h: n0_ref[b][h:h + 1, :])
    i_col = each(lambda b, h: gcol[b][:, h:h + 1])
    i_row = each(lambda b, h: grow[b][h:h + 1, :])
    b_col = each(lambda b, h: bcol[b][:, N_HEADS + h:N_HEADS + h + 1])
    b_row = each(lambda b, h: brow[b][N_HEADS + h:N_HEADS + h + 1, :])
    m_prev = each(lambda b, h: m0_ref[b][h:h + 1, 0:1])
    n = range(len(items))

    qk = [_dot_nt(q[j], k[j]) for j in n]
    qc = [_dot(q[j].astype(BF16), c_old[j].astype(BF16)) for j in n]
    qn = [jnp.sum(q[j] * n_old[j], axis=1, keepdims=True) for j in n]
    dm = [jnp.where(causal, b_col[j] - b_row[j] + i_row[j], -jnp.inf) for j in n]
    a_col = [b_col[j] + m_prev[j] for j in n]
    m_t = [jnp.maximum(a_col[j], jnp.max(dm[j], axis=1, keepdims=True)) for j in n]
    s = [qk[j] * jnp.exp(dm[j] - m_t[j]) for j in n]
    w_inter = [jnp.exp(a_col[j] - m_t[j]) for j in n]
    num = [_dot(s[j], v[j]) + w_inter[j] * qc[j] for j in n]
    den = [jnp.sum(s[j], axis=1, keepdims=True) + w_inter[j] * qn[j] for j in n]
    hh = [num[j] / jnp.maximum(jnp.abs(den[j]), jnp.exp(-m_t[j])) for j in n]
    ms = [jnp.mean(hh[j] * hh[j], axis=-1, keepdims=True) for j in n]
    for j, (b, h) in enumerate(items):
        h_ref[b, :, sl(h)] = hh[j] * lax.rsqrt(ms[j] + NORM_EPS) * gm_ref[:, sl(h)] * o_ref[b][:, sl(h)]

    m_new = [m_t[j][L - 1:L, :] for j in n]
    b_last = [b_col[j][L - 1:L, :] for j in n]
    decay = [jnp.exp(b_last[j] + m_prev[j] - m_new[j]) for j in n]
    ws_row = [jnp.exp(b_last[j] - b_row[j] + i_row[j] - m_new[j]) for j in n]
    ws_col = [jnp.exp(b_last[j] - b_col[j] + i_col[j] - m_new[j]) for j in n]
    kv = [_dot(kT_ref[b][sl(h), :] * ws_row[j], v[j]) for j, (b, h) in enumerate(items)]
    for j, (b, h) in enumerate(items):
        c_out_ref[b, h] = decay[j] * c_old[j] + kv[j]
        n_out_ref[b, h:h + 1, :] = decay[j] * n_old[j] + jnp.sum(k[j] * ws_col[j], axis=0, keepdims=True)
        m_out_ref[b, h:h + 1, :] = jnp.broadcast_to(m_new[j], (1, HEAD_DIM))


def _mlstm_sample(q, k, kT, v, o, gcol, grow, c0, n0, m0, g_mhead):
    db, L, _ = q.shape
    sb = SEQS_PER_STEP if db % SEQS_PER_STEP == 0 else 1
    seq = lambda *shape: pl.BlockSpec((sb,) + shape, lambda b: (b,) + (0,) * len(shape))
    m0_rep = jnp.broadcast_to(m0[:, :, None], (db, N_HEADS, HEAD_DIM))
    nbytes = sb * (2 * 2 * _nbytes((N_HEADS, HEAD_DIM, HEAD_DIM), F32) + 16 * _nbytes((8, WIDTH), F32))
    outs = pl.pallas_call(
        _mlstm_sample_kernel,
        grid=(db // sb,),
        in_specs=[seq(L, WIDTH), seq(L, WIDTH), seq(WIDTH, L), seq(L, WIDTH), seq(L, WIDTH),
                  seq(L, GATE_LANES), seq(GATE_ROWS, L),
                  seq(N_HEADS, HEAD_DIM, HEAD_DIM), seq(N_HEADS, HEAD_DIM), seq(N_HEADS, HEAD_DIM),
                  _const_spec((1, WIDTH))],
        out_specs=[seq(L, WIDTH), seq(N_HEADS, HEAD_DIM, HEAD_DIM), seq(N_HEADS, HEAD_DIM),
                   seq(N_HEADS, HEAD_DIM)],
        out_shape=[jax.ShapeDtypeStruct((db, L, WIDTH), F32),
                   jax.ShapeDtypeStruct((db, N_HEADS, HEAD_DIM, HEAD_DIM), F32),
                   jax.ShapeDtypeStruct((db, N_HEADS, HEAD_DIM), F32),
                   jax.ShapeDtypeStruct((db, N_HEADS, HEAD_DIM), F32)],
        compiler_params=_params(("parallel",), nbytes),
        name="mlstm_sample",
    )(q, k, kT, v, o, gcol, grow, c0, n0, m0_rep, g_mhead.reshape(1, WIDTH))
    hm, c_new, n_new, m_new = outs
    return hm, c_new, n_new, m_new[:, :, 0]


def _rel_bucket(dist):
    max_exact = REL_BUCKETS // 2
    n = jnp.maximum(dist, 0)
    nf = jnp.maximum(n, 1).astype(F32)
    large = max_exact + (jnp.log(nf / max_exact) / math.log(REL_MAX_DIST / max_exact)
                         * (REL_BUCKETS - max_exact)).astype(jnp.int32)
    large = jnp.minimum(large, REL_BUCKETS - 1)
    return jnp.where(n < max_exact, n, large)


def _bias_of_bucket(bucket, rb_ref, h):
    out = jnp.zeros(bucket.shape, F32)
    for b in range(REL_BUCKETS):
        out = jnp.where(bucket == b, rb_ref[b * N_HEADS + h], out)
    return out


def _rel_bias_from_dist(dist, rb_ref, h):
    return _bias_of_bucket(_rel_bucket(dist), rb_ref, h)


def _prompt_bias_kernel(rb_ref, o_ref):
    delta = pl.program_id(0)
    s = lax.broadcasted_iota(jnp.int32, (MOBA_BLOCK, MOBA_BLOCK), 0)
    t = lax.broadcasted_iota(jnp.int32, (MOBA_BLOCK, MOBA_BLOCK), 1)
    dist = delta * MOBA_BLOCK + t - s
    bucket = _rel_bucket(dist)
    for h in range(N_HEADS):
        bias = _bias_of_bucket(bucket, rb_ref, h) - rb_ref[(REL_BUCKETS - 1) * N_HEADS + h]
        o_ref[0, h] = jnp.where(dist >= 0, bias * LOG2E, NEG_INF)


def _near_blocks():
    return -(-(REL_SAT + MOBA_BLOCK - 1) // MOBA_BLOCK)


def _prompt_bias(rel_bias):
    nd = _near_blocks() + 1
    return pl.pallas_call(
        _prompt_bias_kernel,
        grid=(nd,),
        in_specs=[pl.BlockSpec(memory_space=pltpu.SMEM)],
        out_specs=pl.BlockSpec((1, N_HEADS, MOBA_BLOCK, MOBA_BLOCK), lambda d: (d, 0, 0, 0)),
        out_shape=jax.ShapeDtypeStruct((nd, N_HEADS, MOBA_BLOCK, MOBA_BLOCK), F32),
        compiler_params=_params(("parallel",), 4 * N_HEADS * _nbytes((MOBA_BLOCK, MOBA_BLOCK), F32)),
        name="prompt_bias",
    )(rel_bias.reshape(-1))


def _top_mask(sc, valid, ksel, axis):
    n = sc.shape[axis]
    iota = lax.broadcasted_iota(jnp.int32, sc.shape, axis)
    cur = jnp.where(valid, sc, NEG_INF)
    sel = jnp.zeros(sc.shape, F32)
    for _ in range(ksel):
        mx = jnp.max(cur, axis=axis, keepdims=True)
        idx = jnp.min(jnp.where(cur == mx, iota, n), axis=axis, keepdims=True)
        hit = iota == idx
        sel = jnp.where(hit, 1.0, sel)
        cur = jnp.where(hit, -jnp.inf, cur)
    return jnp.where((sel > 0.0) & valid, 0.0, NEG_INF)


def _mask_rows(nb):
    return -(-nb // 16) * 16


def _prompt_select_kernel(ksel, kmean_ref, qT_ref, o_ref):
    tq = qT_ref.shape[2]
    nb = kmean_ref.shape[0]
    nbp = o_ref.shape[2] - HEAD_DIM
    q0 = pl.program_id(0) * tq
    own = (q0 + lax.broadcasted_iota(jnp.int32, (nb, tq), 1)) // MOBA_BLOCK
    blk = lax.broadcasted_iota(jnp.int32, (nb, tq), 0)
    for h in range(N_HEADS):
        sl = slice(h * HEAD_DIM, (h + 1) * HEAD_DIM)
        q = qT_ref[0, sl, :]
        sc = jnp.dot(kmean_ref[:, sl], q, precision=HIGHEST, preferred_element_type=F32)
        mask = jnp.where(blk == own, 0.0, _top_mask(sc, blk < own, ksel, 0))
        o_ref[0, h, 0:HEAD_DIM, :] = (q * (HEAD_DIM ** -0.5 * LOG2E)).astype(BF16)
        o_ref[0, h, HEAD_DIM:HEAD_DIM + nb, :] = mask.astype(BF16)
        if nbp > nb:
            o_ref[0, h, HEAD_DIM + nb:, :] = jnp.zeros((nbp - nb, tq), BF16)


def _prompt_select(kmean, qT):
    nb = kmean.shape[0]
    nt, _, tq = qT.shape
    ksel = min(MOBA_TOPK, nb - 1)
    rows = HEAD_DIM + _mask_rows(nb)
    return pl.pallas_call(
        functools.partial(_prompt_select_kernel, ksel),
        grid=(nt,),
        in_specs=[_const_spec((nb, WIDTH)), pl.BlockSpec((1, WIDTH, tq), lambda i: (i, 0, 0))],
        out_specs=pl.BlockSpec((1, N_HEADS, rows, tq), lambda i: (i, 0, 0, 0)),
        out_shape=jax.ShapeDtypeStruct((nt, N_HEADS, rows, tq), BF16),
        compiler_params=_params(("parallel",), 4 * _nbytes((WIDTH, tq), F32)),
        name="prompt_select",
    )(kmean, qT)


def _moba_prompt_kernel(n_bias, n_steps, qi_ref, gi_ref, last_ref, qa_ref, k_ref, vT_ref, bias_ref, o_ref):
    g_blocks = KV_GROUP
    rows = g_blocks * MOBA_BLOCK
    tq = qa_ref.shape[3]

    def logits(step):
        i, g = qi_ref[step], gi_ref[step]
        start = pl.multiple_of(g * rows, rows)
        sT = _dot(k_ref[0, pl.ds(start, rows), :], qa_ref[i, 0])
        sT = jnp.concatenate(
            [sT[b * MOBA_BLOCK:(b + 1) * MOBA_BLOCK]
             + bias_ref[jnp.clip(i - (g * g_blocks + b), 0, n_bias - 1), 0] for b in range(g_blocks)], axis=0)
        return sT, jnp.max(sT, axis=0, keepdims=True)

    def body(step, carry):
        m, l, acc, sT, mx = carry
        nxt = logits(step + 1)
        i, g = qi_ref[step], gi_ref[step]
        m_new = jnp.maximum(m, mx)
        alpha = jnp.exp2(m - m_new)
        l, acc = alpha * l, alpha * acc
        for b in range(g_blocks):
            p = jnp.exp2(sT[b * MOBA_BLOCK:(b + 1) * MOBA_BLOCK] - m_new)
            l = l + jnp.sum(p, axis=0, keepdims=True)
            acc = acc + _dot(vT_ref[g * g_blocks + b], p.astype(BF16))
        o_ref[pl.ds(pl.multiple_of(i * MOBA_BLOCK, MOBA_BLOCK), MOBA_BLOCK), :] = (
            acc / jnp.where(l > 0.0, l, 1.0)).T
        last = last_ref[step] == 1
        return (jnp.where(last, NEG_INF, m_new), jnp.where(last, 0.0, l), jnp.where(last, 0.0, acc)) + nxt

    init = (jnp.full((1, tq), NEG_INF, F32), jnp.zeros((1, tq), F32), jnp.zeros((HEAD_DIM, tq), F32)) + logits(0)
    lax.fori_loop(0, n_steps, body, init, unroll=2)


def _triangle_schedule(nb):
    qi, gi, last = [], [], []
    for i in range(nb):
        n_groups = i // KV_GROUP + 1
        for g in range(n_groups):
            qi.append(i)
            gi.append(g)
            last.append(int(g == n_groups - 1))
    n_steps = len(qi)
    as_i32 = lambda a: jnp.asarray(a + a[-1:], jnp.int32)
    return n_steps, as_i32(qi), as_i32(gi), as_i32(last)


def _moba_prompt(qa, k_aug, vT_bf, bias):
    nb, _, ka, _ = qa.shape
    t = k_aug.shape[1]
    n_bias = bias.shape[0]
    assert t == nb * MOBA_BLOCK and nb % KV_GROUP == 0 and vT_bf.shape == (nb, WIDTH, MOBA_BLOCK)
    n_steps, qi, gi, last = _triangle_schedule(nb)
    once = dict(pipeline_mode=pl.Buffered(1))
    nbytes = (_nbytes((t, -(-ka // 128) * 128), BF16) + _nbytes((t, ka), BF16) + _nbytes((t, HEAD_DIM), BF16)
              + _nbytes((n_bias, MOBA_BLOCK, MOBA_BLOCK), F32) + 2 * _nbytes((t, HEAD_DIM), F32)
              + 12 * KV_GROUP * _nbytes((MOBA_BLOCK, MOBA_BLOCK), F32))
    return pl.pallas_call(
        functools.partial(_moba_prompt_kernel, n_bias, n_steps),
        grid_spec=pltpu.PrefetchScalarGridSpec(
            num_scalar_prefetch=3,
            grid=(N_HEADS,),
            in_specs=[pl.BlockSpec((nb, 1, ka, MOBA_BLOCK), lambda h, *_: (0, h, 0, 0), **once),
                      pl.BlockSpec((1, t, ka), lambda h, *_: (h, 0, 0), **once),
                      pl.BlockSpec((nb, HEAD_DIM, MOBA_BLOCK), lambda h, *_: (0, h, 0), **once),
                      pl.BlockSpec((n_bias, 1, MOBA_BLOCK, MOBA_BLOCK), lambda h, *_: (0, h, 0, 0), **once)],
            out_specs=pl.BlockSpec((t, HEAD_DIM), lambda h, *_: (0, h))),
        out_shape=jax.ShapeDtypeStruct((t, WIDTH), F32),
        compiler_params=_params(("arbitrary",), nbytes),
        name="moba_prompt",
    )(qi, gi, last, qa, k_aug, vT_bf, bias)


def _sample_bias_kernel(past_len, t_new, rb_ref, past_ref, page_ref, new_ref):
    page = pl.program_id(0)

    def table(n_cols, first_kpos, masked):
        shape = (N_HEADS * t_new, n_cols)
        r = lax.broadcasted_iota(jnp.int32, shape, 0)
        c = lax.broadcasted_iota(jnp.int32, shape, 1)
        tpos = past_len + r % t_new
        kpos = first_kpos + c // N_HEADS
        bucket = _rel_bucket(tpos - kpos)
        ok = (kpos >= (tpos // MOBA_BLOCK) * MOBA_BLOCK) & (kpos <= tpos) if masked else None
        out = jnp.full(shape, NEG_INF, F32)
        for h in range(N_HEADS):
            same = (r // t_new == h) & (c % N_HEADS == h)
            out = jnp.where(same if ok is None else same & ok, _bias_of_bucket(bucket, rb_ref, h) * LOG2E, out)
        return out

    for u in range(past_ref.shape[0]):
        past_ref[u] = table(PAGE_ROWS, (page * past_ref.shape[0] + u) * PAGE_SIZE, False)

    @pl.when(page == 0)
    def _():
        page_ref[...] = table(PAGE_ROWS, past_len - PAGE_SIZE, True)
        new_ref[...] = table(new_ref.shape[1], past_len, True)


def _sample_bias(rel_bias, past_len, t_new):
    rows = N_HEADS * t_new
    n_pages = past_len // PAGE_SIZE
    new_cols = t_new * N_HEADS
    pg = math.gcd(n_pages, 8)
    return pl.pallas_call(
        functools.partial(_sample_bias_kernel, past_len, t_new),
        grid=(n_pages // pg,),
        in_specs=[pl.BlockSpec(memory_space=pltpu.SMEM)],
        out_specs=[pl.BlockSpec((pg, rows, PAGE_ROWS), lambda p: (p, 0, 0)),
                   pl.BlockSpec((rows, PAGE_ROWS), lambda p: (0, 0)),
                   pl.BlockSpec((rows, new_cols), lambda p: (0, 0))],
        out_shape=[jax.ShapeDtypeStruct((n_pages, rows, PAGE_ROWS), F32),
                   jax.ShapeDtypeStruct((rows, PAGE_ROWS), F32),
                   jax.ShapeDtypeStruct((rows, new_cols), F32)],
        compiler_params=_params(("arbitrary",), 16 * _nbytes((rows, PAGE_ROWS), F32)),
        name="sample_bias",
    )(rel_bias.reshape(-1))


def _page_specs(pps, n_pages):
    def spec(u):
        return pl.BlockSpec((1, PAGE_ROWS, HEAD_DIM), lambda b, c, pt: (pt[b * n_pages + c * pps + u], 0, 0))
    return [spec(u) for u in range(pps)]


def _softmax_partial(logit_tiles, value_tiles):
    m = jnp.max(logit_tiles[0], axis=1, keepdims=True)
    for lg in logit_tiles[1:]:
        m = jnp.maximum(m, jnp.max(lg, axis=1, keepdims=True))
    l, o = 0.0, 0.0
    for lg, v in zip(logit_tiles, value_tiles):
        p = jnp.exp2(lg - m)
        l = l + jnp.sum(p, axis=1, keepdims=True)
        o = o + _dot(p.astype(BF16), v.astype(BF16))
    return m, l, o


def _moba_sample_kernel(pps, n_blocks, t_new, pt_ref, q_ref, knew_ref, vnew_ref,
                        bpast_ref, bpage_ref, bnew_ref, *refs):
    k_refs, v_refs = refs[:pps], refs[pps:2 * pps]
    o_ref = refs[2 * pps]
    sc_s, m_s, l_s, o_s = refs[2 * pps + 1:]
    del pt_ref
    c = pl.program_id(1)
    nch = pl.num_programs(1)
    ppb = MOBA_BLOCK // PAGE_SIZE
    bps = pps // ppb
    rows = N_HEADS * t_new
    fold = 8 // N_HEADS
    ksel = min(MOBA_TOPK, n_blocks)
    q = q_ref[0]
    qb = (q * (HEAD_DIM ** -0.5 * LOG2E)).astype(BF16)
    lanes = lambda a: jnp.broadcast_to(a, (rows, HEAD_DIM))

    pages = [k_refs[u][0] for u in range(pps)]
    tiles = [_dot_nt(qb, pages[u].astype(BF16)) + bpast_ref[c * pps + u] for u in range(pps)]
    for b in range(bps):
        n = c * bps + b
        ksum = sum(jnp.sum(pages[b * ppb + u].reshape(PAGE_ROWS // 8, 8, HEAD_DIM), axis=0)
                   for u in range(ppb))
        kmean = sum(ksum[f * N_HEADS:(f + 1) * N_HEADS] for f in range(fold)) / MOBA_BLOCK
        kmean_rows = jnp.concatenate(
            [jnp.broadcast_to(kmean[h:h + 1], (t_new, HEAD_DIM)) for h in range(N_HEADS)], axis=0)
        sc_s[n] = lanes(jnp.sum(q * kmean_rows, axis=1, keepdims=True))
        m, l, o = _softmax_partial(tiles[b * ppb:(b + 1) * ppb], [v_refs[b * ppb + u][0] for u in range(ppb)])
        m_s[n], l_s[n], o_s[n] = lanes(m), lanes(l), o

    @pl.when(c == nch - 1)
    def _():
        own_page = _dot_nt(qb, pages[pps - 1].astype(BF16)) + bpage_ref[...]
        m_pg, l_pg, o_pg = _softmax_partial([own_page], [v_refs[pps - 1][0]])
        new = _dot_nt(qb, knew_ref[0].astype(BF16)) + bnew_ref[...]
        m_nw, l_nw, o_nw = _softmax_partial([new], [vnew_ref[0]])

        lane = lax.broadcasted_iota(jnp.int32, (rows, HEAD_DIM), 1)
        r = lax.broadcasted_iota(jnp.int32, (rows, HEAD_DIM), 0)
        sc, mm, ll = jnp.zeros((rows, HEAD_DIM), F32), jnp.zeros((rows, HEAD_DIM), F32), jnp.zeros((rows, HEAD_DIM), F32)
        for n in range(n_blocks):
            at = lane == n
            sc, mm, ll = jnp.where(at, sc_s[n], sc), jnp.where(at, m_s[n], mm), jnp.where(at, l_s[n], ll)
        tpos = n_blocks * MOBA_BLOCK + r % t_new
        valid = (lane < tpos // MOBA_BLOCK) & (lane < n_blocks)
        mm = jnp.where(lane < n_blocks, mm + _top_mask(sc, valid, ksel, 1), NEG_INF)
        m_all = jnp.maximum(jnp.max(mm, axis=1, keepdims=True), jnp.maximum(m_pg, m_nw))
        w = jnp.exp2(mm - m_all)
        w_pg, w_nw = jnp.exp2(m_pg - m_all), jnp.exp2(m_nw - m_all)
        den = jnp.sum(w * ll, axis=1, keepdims=True) + w_pg * l_pg + w_nw * l_nw
        num = w_pg * o_pg + w_nw * o_nw
        for n in range(n_blocks):
            num = num + jnp.sum(jnp.where(lane == n, w, 0.0), axis=1, keepdims=True) * o_s[n]
        o_ref[0] = num / den


def _moba_sample(q_rows, k_new, v_new, cache_k, cache_v, page_table, bias_past, bias_page, bias_new):
    db, rows, _ = q_rows.shape
    t_new = rows // N_HEADS
    n_pages = page_table.shape[1]
    ppb = MOBA_BLOCK // PAGE_SIZE
    assert n_pages % ppb == 0 and 8 % N_HEADS == 0 and t_new % 8 == 0
    n_blocks = n_pages // ppb
    assert n_blocks <= HEAD_DIM, "per-block scalars are gathered into one lane tile"
    pps = min(PAGES_PER_STEP, n_pages)
    assert n_pages % pps == 0 and pps % ppb == 0
    seq = lambda *shape: pl.BlockSpec((1,) + shape, lambda b, c, pt: (b,) + (0,) * len(shape))
    slab = pltpu.VMEM((n_blocks, rows, HEAD_DIM), F32)
    nbytes = (4 * pps * _nbytes((PAGE_ROWS, HEAD_DIM), F32) + _nbytes(bias_past.shape, F32)
              + 4 * _nbytes((n_blocks, rows, HEAD_DIM), F32) + 64 * _nbytes((rows, PAGE_ROWS), F32))
    return pl.pallas_call(
        functools.partial(_moba_sample_kernel, pps, n_blocks, t_new),
        grid_spec=pltpu.PrefetchScalarGridSpec(
            num_scalar_prefetch=1,
            grid=(db, n_pages // pps),
            in_specs=[seq(rows, HEAD_DIM), seq(rows, HEAD_DIM), seq(rows, HEAD_DIM),
                      _const_spec(bias_past.shape), _const_spec(bias_page.shape), _const_spec(bias_new.shape)]
                     + _page_specs(pps, n_pages) + _page_specs(pps, n_pages),
            out_specs=seq(rows, HEAD_DIM),
            scratch_shapes=[slab, slab, slab, slab]),
        out_shape=jax.ShapeDtypeStruct((db, rows, HEAD_DIM), F32),
        compiler_params=_params(("parallel", "arbitrary"), nbytes),
        name="moba_sample",
    )(page_table.reshape(-1), q_rows, k_new, v_new, bias_past, bias_page, bias_new,
      *([cache_k] * pps), *([cache_v] * pps))


def _out_ffn_kernel(ff_chunk, x_ref, hm_ref, ha_ref, sgm_ref, sga_ref, gate1_ref, shift2_ref, scale2_ref,
                    gate2_ref, g_ref, wbm_ref, wba_ref, wout_ref, wup_ref, wdn_ref, y_ref):
    bm = _dot(hm_ref[...].astype(BF16), wbm_ref[...])
    ba = _dot(ha_ref[...].astype(BF16), wba_ref[...])
    mix = sgm_ref[...] * bm + sga_ref[...] * ba
    x1 = x_ref[...] + gate1_ref[...] * _dot(mix.astype(BF16), wout_ref[...])
    y = x1 * lax.rsqrt(jnp.mean(x1 * x1, axis=-1, keepdims=True) + NORM_EPS) * g_ref[...]
    h2 = (y * (1.0 + scale2_ref[...]) + shift2_ref[...]).astype(BF16)
    d_ff = wup_ref.shape[1]
    acc = jnp.zeros(x1.shape, F32)
    for c in range(d_ff // ff_chunk):
        u = jnp.maximum(_dot(h2, wup_ref[:, c * ff_chunk:(c + 1) * ff_chunk]), 0.0)
        acc = acc + _dot((u * u).astype(BF16), wdn_ref[c * ff_chunk:(c + 1) * ff_chunk, :])
    y_ref[...] = x1 + gate2_ref[...] * acc


def _out_ffn(x, hm, ha, sgm, sga, gate1, shift2, scale2, gate2, g_norm, wbm, wba, wout, wup, wdn, tm):
    t, d = x.shape
    per_tok = gate1.shape[0] != 1
    tok = lambda n: pl.BlockSpec((tm, n), lambda i: (i, 0))
    mod = tok(d) if per_tok else pl.BlockSpec((1, d), lambda i: (0, 0))
    weights = (wbm, wba, wout, wup, wdn)
    nbytes = (sum(_nbytes(w.shape, w.dtype) for w in weights)
              + 2 * _nbytes((tm, d), F32) * (5 + (4 if per_tok else 0)) + 8 * _nbytes((tm, d), F32))
    return pl.pallas_call(
        functools.partial(_out_ffn_kernel, 1024),
        grid=(t // tm,),
        in_specs=[tok(d), tok(WIDTH), tok(WIDTH), tok(d), tok(d), mod, mod, mod, mod, _const_spec((1, d))]
                 + [_const_spec(w.shape) for w in weights],
        out_specs=tok(d),
        out_shape=jax.ShapeDtypeStruct((t, d), F32),
        compiler_params=_params(("parallel",), nbytes),
        name="out_ffn",
    )(x, hm, ha, sgm, sga, gate1, shift2, scale2, gate2, g_norm.reshape(1, d), *weights)


def _layer(x_prompt, x_sample, cache_k, cache_v, state_c, state_n, state_m, page_table, c_prompt, c_sample,
           rel_bias, w_ada, b_ada, g_norm_mix, w_in, b_igate, b_fgate, g_mhead, g_qnorm, g_knorm,
           w_branch_m, w_branch_a, w_out, g_norm_ffn, w_ff_up, w_ff_down):
    bsz, seq, d = x_prompt.shape
    db, t_new, _ = x_sample.shape
    assert bsz == 1, "the prompt path handles one sequence"
    n_pages = page_table.shape[1]
    past_len = n_pages * PAGE_SIZE
    k_scale = HEAD_DIM ** -0.5

    c_all = jnp.concatenate([c_prompt, c_sample], axis=0)
    pad = (-c_all.shape[0]) % 8
    mod = _ada(jnp.pad(c_all, ((0, pad), (0, 0))), w_ada, b_ada)
    mod_p = jnp.split(mod[0:1], N_COND, axis=-1)
    mod_s = jnp.split(jnp.repeat(mod[1:1 + db], t_new, axis=0), N_COND, axis=-1)

    cols = _split_w_in(w_in)
    gate_col, gate_row = _gate_groups(cols, b_igate, b_fgate)
    gq_row, gk_row = g_qnorm.reshape(1, WIDTH), g_knorm.reshape(1, WIDTH)
    gq_col = g_qnorm.reshape(WIDTH, 1)
    wbm, wba, wout = w_branch_m.astype(BF16), w_branch_a.astype(BF16), w_out.astype(BF16)
    wup, wdn = w_ff_up.astype(BF16), w_ff_down.astype(BF16)

    plan_p = [
        _group(cols["mq"], True), _group(cols["mk"], False, "scale", const=k_scale), _group(cols["mv"], True),
        _group(cols["mo"], True, "sigmoid"), gate_col, gate_row,
        _group(cols["aq"], True, "norm_col", auxv=gq_col),
        _group(cols["ak"], False, "norm_row", auxv=gk_row, outs=("heads", BF16, "mean")),
        _group(cols["av"], False, outs=("heads",)), _group(cols["av"], True, outs=(BF16,)),
        _group(cols["gm"], False, "sigmoid"), _group(cols["ga"], False, "sigmoid"),
    ]
    xp = x_prompt.reshape(seq, d)
    (mqT, mk, mvT, soT, gcol, grow, aqT, ak, ak_bf, kmean, av, avT_bf, sgm, sga) = _in_proj(
        xp, mod_p[0], mod_p[1], g_norm_mix, plan_p, TOKEN_TILE)
    hm, c_p, n_p, m_p = _mlstm_prompt(mqT, mk, mvT, soT, gcol, grow, g_mhead)
    nb = seq // MOBA_BLOCK
    qa = _prompt_select(kmean.reshape(nb, WIDTH), aqT)
    block_id = jnp.arange(seq, dtype=jnp.int32)[:, None] // MOBA_BLOCK
    one_hot = (block_id == jnp.arange(_mask_rows(nb), dtype=jnp.int32)[None, :]).astype(BF16)
    k_aug = jnp.concatenate([ak_bf.reshape(seq, N_HEADS, HEAD_DIM).transpose(1, 0, 2),
                             jnp.broadcast_to(one_hot, (N_HEADS,) + one_hot.shape)], axis=-1)
    ha = _moba_prompt(qa, k_aug, avT_bf, _prompt_bias(rel_bias))
    y_p = _out_ffn(xp, hm, ha, sgm, sga, mod_p[2], mod_p[3], mod_p[4], mod_p[5], g_norm_ffn,
                   wbm, wba, wout, wup, wdn, TOKEN_TILE)

    plan_s = [
        _group(cols["mq"], False), _group(cols["mk"], False, "scale", const=k_scale), _group(cols["mv"], False),
        _group(cols["mo"], False, "sigmoid"), gate_col, gate_row,
        _group(cols["aq"], False, "norm_row", auxv=gq_row),
        _group(cols["ak"], False, "norm_row", auxv=gk_row, outs=("heads",)),
        _group(cols["av"], False, outs=("heads",)),
        _group(cols["gm"], False, "sigmoid"), _group(cols["ga"], False, "sigmoid"),
    ]
    ts = db * t_new
    xs = x_sample.reshape(ts, d)
    (mq_s, mk_s, mv_s, so_s, gcol_s, grow_s, aq_s, ak_s, av_s, sgm_s, sga_s) = _in_proj(
        xs, mod_s[0], mod_s[1], g_norm_mix, plan_s, min(TOKEN_TILE, ts))
    per_seq = lambda a: a.reshape(db, t_new, a.shape[-1])
    mk_seq = per_seq(mk_s)
    grow_seq = jnp.transpose(grow_s, (1, 0, 2)).reshape(GATE_ROWS, db, t_new).transpose(1, 0, 2)
    hm_s, c_s, n_s, m_s = _mlstm_sample(per_seq(mq_s), mk_seq, jnp.swapaxes(mk_seq, 1, 2), per_seq(mv_s),
                                        per_seq(so_s), per_seq(gcol_s), grow_seq,
                                        state_c, state_n, state_m, g_mhead)
    rows = N_HEADS * t_new
    ck = cache_k.reshape(cache_k.shape[0], PAGE_ROWS, HEAD_DIM)
    cv = cache_v.reshape(cache_v.shape[0], PAGE_ROWS, HEAD_DIM)
    head_major = lambda a: a.reshape(db, t_new, N_HEADS, HEAD_DIM).transpose(0, 2, 1, 3).reshape(db, rows, HEAD_DIM)
    b_past, b_page, b_new = _sample_bias(rel_bias, past_len, t_new)
    ha_rows = _moba_sample(head_major(aq_s), ak_s.reshape(db, rows, HEAD_DIM), av_s.reshape(db, rows, HEAD_DIM),
                           ck, cv, page_table, b_past, b_page, b_new)
    ha_s = ha_rows.reshape(db, N_HEADS, t_new, HEAD_DIM).transpose(0, 2, 1, 3)
    y_s = _out_ffn(xs, hm_s.reshape(ts, WIDTH), ha_s.reshape(ts, WIDTH), sgm_s, sga_s,
                   mod_s[2], mod_s[3], mod_s[4], mod_s[5], g_norm_ffn, wbm, wba, wout, wup, wdn,
                   min(TOKEN_TILE, ts))

    heads = lambda a, lead: a.reshape(lead + (N_HEADS, HEAD_DIM))
    return (y_p.reshape(bsz, seq, d), y_s.reshape(db, t_new, d),
            heads(ak, (bsz, seq)), heads(av, (bsz, seq)), c_p[None], n_p[None], m_p[None],
            heads(ak_s, (db, t_new)), heads(av_s, (db, t_new)), c_s, n_s, m_s)


def kernel(x_prompt, x_sample, cache_k, cache_v, state_C, state_n, state_m, page_table, c_prompt, c_sample,
           rel_bias, w_ada, b_ada, g_norm_mix, w_in, b_igate, b_fgate, g_mhead, g_qnorm, g_knorm,
           w_branch_m, w_branch_a, w_out, g_norm_ffn, w_ff_up, w_ff_down):
    depth = w_in.shape[0]
    assert depth == 1, "state outputs are stacked per layer; this kernel implements the single-layer trunk"
    outs = _layer(x_prompt, x_sample, cache_k[0], cache_v[0], state_C[0], state_n[0], state_m[0], page_table,
                  c_prompt, c_sample, rel_bias, w_ada[0], b_ada[0], g_norm_mix[0], w_in[0], b_igate[0],
                  b_fgate[0], g_mhead[0], g_qnorm[0], g_knorm[0], w_branch_m[0], w_branch_a[0], w_out[0],
                  g_norm_ffn[0], w_ff_up[0], w_ff_down[0])
    yp, ys, kp, vp, cp, np_, mp, ks, vs, cs, ns, ms = outs
    stack = lambda a: a[None]
    return (yp, ys, stack(kp), stack(vp), stack(cp), stack(np_), stack(mp),
            stack(ks), stack(vs), stack(cs), stack(ns), stack(ms))
```

```python
import functools
import math

import jax
import jax.numpy as jnp
from jax import lax
from jax.experimental import pallas as pl
from jax.experimental.pallas import tpu as pltpu

F32 = jnp.float32
BF16 = jnp.bfloat16
HIGHEST = lax.Precision.HIGHEST

N_HEADS = 4
HEAD_DIM = 128
WIDTH = N_HEADS * HEAD_DIM
MOBA_BLOCK = 256
MOBA_TOPK = 3
PAGE_SIZE = 128
REL_BUCKETS = 32
REL_MAX_DIST = 4096
N_COND = 6
NORM_EPS = 1e-6
NEG_INF = -1e30
LOG2E = math.log2(math.e)
GATE_LANES = 128
GATE_ROWS = 16

V7X_VMEM_BYTES = 64 * 1024 * 1024
VMEM_HEADROOM_BYTES = 8 * 1024 * 1024
TOKEN_TILE = 256
assert TOKEN_TILE == MOBA_BLOCK, "feature-major projection tiles double as MoBA key/query blocks"
PAGES_PER_STEP = 32
KV_GROUP = 4
SEQS_PER_STEP = 2
PAGE_ROWS = PAGE_SIZE * N_HEADS

REL_SAT = 3072


def _vmem_limit(nbytes):
    return int(min(V7X_VMEM_BYTES - VMEM_HEADROOM_BYTES, nbytes + VMEM_HEADROOM_BYTES))


def _nbytes(shape, dtype):
    return math.prod(shape) * jnp.dtype(dtype).itemsize


def _params(sem, nbytes):
    return pltpu.CompilerParams(dimension_semantics=sem, vmem_limit_bytes=_vmem_limit(nbytes))


def _const_spec(shape):
    nd = len(shape)
    return pl.BlockSpec(shape, lambda *_: (0,) * nd, pipeline_mode=pl.Buffered(1))


def _dot(a, b):
    return jnp.dot(a, b, preferred_element_type=F32)


def _dot_nt(a, b):
    return lax.dot_general(a, b, (((1,), (1,)), ((), ())), preferred_element_type=F32)


def _log_sigmoid(x):
    return -(jnp.maximum(-x, 0.0) + jnp.log1p(jnp.exp(-jnp.abs(x))))


def _ada_kernel(c_ref, w_ref, b_ref, o_ref):
    c = c_ref[...]
    s = (c * jax.nn.sigmoid(c)).astype(BF16)
    o_ref[...] = _dot(s, w_ref[...].astype(BF16)) + b_ref[...]


def _ada(c_all, w_ada, b_ada):
    rows, d = c_all.shape
    n_out = w_ada.shape[1]
    tn = d
    nbytes = 2 * (_nbytes((rows, d), F32) + _nbytes((d, tn), F32) + _nbytes((rows, tn), F32))
    return pl.pallas_call(
        _ada_kernel,
        grid=(n_out // tn,),
        in_specs=[pl.BlockSpec((rows, d), lambda j: (0, 0)),
                  pl.BlockSpec((d, tn), lambda j: (0, j)),
                  pl.BlockSpec((1, tn), lambda j: (0, j))],
        out_specs=pl.BlockSpec((rows, tn), lambda j: (0, j)),
        out_shape=jax.ShapeDtypeStruct((rows, n_out), F32),
        compiler_params=_params(("arbitrary",), nbytes),
        name="ada",
    )(c_all, w_ada, b_ada.reshape(1, n_out))


def _head_norm_rows(z, g):
    outs = []
    for h in range(N_HEADS):
        blk = z[:, h * HEAD_DIM:(h + 1) * HEAD_DIM]
        ms = jnp.mean(blk * blk, axis=-1, keepdims=True)
        outs.append(blk * lax.rsqrt(ms + NORM_EPS) * g[:, h * HEAD_DIM:(h + 1) * HEAD_DIM])
    return jnp.concatenate(outs, axis=-1)


def _head_norm_cols(z, g):
    outs = []
    for h in range(N_HEADS):
        blk = z[h * HEAD_DIM:(h + 1) * HEAD_DIM, :]
        ms = jnp.mean(blk * blk, axis=0, keepdims=True)
        outs.append(blk * lax.rsqrt(ms + NORM_EPS) * g[h * HEAD_DIM:(h + 1) * HEAD_DIM, :])
    return jnp.concatenate(outs, axis=0)


def _in_proj_kernel(plan, x_ref, shift_ref, scale_ref, g_ref, *refs):
    n_groups = len(plan)
    w_refs = refs[:n_groups]
    n_aux = sum(1 for p in plan if p["aux"])
    aux_refs = refs[n_groups:n_groups + n_aux]
    out_refs = refs[n_groups + n_aux:]

    x = x_ref[...]
    y = x * lax.rsqrt(jnp.mean(x * x, axis=-1, keepdims=True) + NORM_EPS) * g_ref[...]
    hb = (y * (1.0 + scale_ref[...]) + shift_ref[...]).astype(BF16)

    ai = 0
    oi = 0
    for p, w_ref in zip(plan, w_refs):
        z = _dot_nt(w_ref[...], hb) if p["trans"] else _dot(hb, w_ref[...])
        aux = None
        if p["aux"]:
            aux = aux_refs[ai][...]
            ai += 1
        epi = p["epi"]
        if epi == "scale":
            z = z * p["const"]
        elif epi == "sigmoid":
            z = jax.nn.sigmoid(z)
        elif epi == "norm_row":
            z = _head_norm_rows(z, aux)
        elif epi == "norm_col":
            z = _head_norm_cols(z, aux)
        elif epi == "gate_col":
            pre = z + aux
            lane = lax.broadcasted_iota(jnp.int32, pre.shape, 1)
            z = jnp.where(lane < N_HEADS, pre, _log_sigmoid(pre))
        elif epi == "gate_row":
            pre = z + aux
            row = lax.broadcasted_iota(jnp.int32, pre.shape, 0)
            z = jnp.where(row < N_HEADS, pre, _log_sigmoid(pre))
        for kind in p["outs"]:
            if kind == "heads":
                for h in range(N_HEADS):
                    out_refs[oi][:, h, :] = z[:, h * HEAD_DIM:(h + 1) * HEAD_DIM]
            elif kind == "mean":
                out_refs[oi][0] = jnp.mean(z, axis=0, keepdims=True)
            elif p["trans"]:
                out_refs[oi][0] = z.astype(kind)
            else:
                out_refs[oi][...] = z.astype(kind)
            oi += 1


def _in_proj(x, shift, scale, g_norm, plan, tm):
    t, d = x.shape
    per_tok = shift.shape[0] != 1
    mod_spec = (pl.BlockSpec((tm, d), lambda i: (i, 0)) if per_tok
                else pl.BlockSpec((1, d), lambda i: (0, 0)))
    in_specs = [pl.BlockSpec((tm, d), lambda i: (i, 0)), mod_spec, mod_spec, _const_spec((1, d))]
    args = [x, shift, scale, g_norm.reshape(1, d)]
    nbytes = 2 * _nbytes((tm, d), F32) * (3 if per_tok else 1)
    for p in plan:
        in_specs.append(_const_spec(p["w"].shape))
        args.append(p["w"])
        nbytes += _nbytes(p["w"].shape, p["w"].dtype)
    for p in plan:
        if p["aux"]:
            in_specs.append(_const_spec(p["auxv"].shape))
            args.append(p["auxv"])
    out_specs, out_shapes = [], []
    for p in plan:
        n = p["w"].shape[0] if p["trans"] else p["w"].shape[1]
        for kind in p["outs"]:
            if kind == "heads":
                assert not p["trans"] and n == WIDTH
                out_specs.append(pl.BlockSpec((tm, N_HEADS, HEAD_DIM), lambda i: (i, 0, 0)))
                out_shapes.append(jax.ShapeDtypeStruct((t, N_HEADS, HEAD_DIM), F32))
            elif kind == "mean":
                assert not p["trans"] and tm == MOBA_BLOCK
                out_specs.append(pl.BlockSpec((1, 1, n), lambda i: (i, 0, 0)))
                out_shapes.append(jax.ShapeDtypeStruct((t // tm, 1, n), F32))
            elif p["trans"]:
                out_specs.append(pl.BlockSpec((1, n, tm), lambda i: (i, 0, 0)))
                out_shapes.append(jax.ShapeDtypeStruct((t // tm, n, tm), kind))
            else:
                out_specs.append(pl.BlockSpec((tm, n), lambda i: (i, 0)))
                out_shapes.append(jax.ShapeDtypeStruct((t, n), kind))
            nbytes += 2 * _nbytes((tm, n), F32)
    plan_static = tuple({k: v for k, v in p.items() if k not in ("w", "auxv")} for p in plan)
    return pl.pallas_call(
        functools.partial(_in_proj_kernel, plan_static),
        grid=(t // tm,),
        in_specs=in_specs,
        out_specs=out_specs,
        out_shape=out_shapes,
        compiler_params=_params(("parallel",), nbytes + 4 * _nbytes((tm, 1024), F32)),
        name="in_proj",
    )(*args)


def _group(w, trans, epi="none", const=None, auxv=None, outs=(F32,)):
    w = (w.T if trans else w).astype(BF16)
    return dict(w=w, trans=trans, epi=epi, const=const, aux=auxv is not None, auxv=auxv, outs=tuple(outs))


def _split_w_in(w_in):
    sizes = (WIDTH, WIDTH, WIDTH, WIDTH, N_HEADS, N_HEADS, WIDTH, WIDTH, WIDTH, w_in.shape[0], w_in.shape[0])
    names = ("mq", "mk", "mv", "mo", "mi", "mf", "aq", "ak", "av", "gm", "ga")
    out, off = {}, 0
    for n, s in zip(names, sizes):
        out[n] = w_in[:, off:off + s]
        off += s
    assert off == w_in.shape[1]
    return out


def _gate_groups(cols, b_igate, b_fgate):
    w_gate = jnp.concatenate([cols["mi"], cols["mf"]], axis=1)
    bias = jnp.concatenate([b_igate, b_fgate]).astype(F32)
    pad_to = lambda a, n, axis: jnp.pad(a, [(0, n - a.shape[i]) if i == axis else (0, 0) for i in range(a.ndim)])
    return (_group(pad_to(w_gate, GATE_LANES, 1), False, "gate_col", auxv=pad_to(bias.reshape(1, -1), GATE_LANES, 1)),
            _group(pad_to(w_gate, GATE_ROWS, 1), True, "gate_row", auxv=pad_to(bias.reshape(-1, 1), GATE_ROWS, 0)))


def _mlstm_prompt_kernel(qT_ref, k_ref, vT_ref, oT_ref, gcol_ref, grow_ref, gm_ref,
                         h_ref, c_out_ref, n_out_ref, m_out_ref, c_s, n_s, m_s):
    step = pl.program_id(0)
    L = k_ref.shape[0]

    @pl.when(step == 0)
    def _():
        c_s[...] = jnp.zeros_like(c_s)
        n_s[...] = jnp.zeros_like(n_s)
        m_s[...] = jnp.zeros_like(m_s)

    r = lax.broadcasted_iota(jnp.int32, (L, L), 0)
    c = lax.broadcasted_iota(jnp.int32, (L, L), 1)
    tril = (c <= r).astype(F32)
    triu = (r <= c)
    gcol = gcol_ref[...]
    grow = grow_ref[0]
    bcol_all = jnp.dot(tril, gcol, precision=HIGHEST, preferred_element_type=F32)
    brow_all = jnp.dot(grow, triu.astype(F32), precision=HIGHEST, preferred_element_type=F32)

    hs = range(N_HEADS)
    sl = lambda h: slice(h * HEAD_DIM, (h + 1) * HEAD_DIM)
    qTb = [qT_ref[0, sl(h), :].astype(BF16) for h in hs]
    k = [k_ref[:, sl(h)] for h in hs]
    vTb = [vT_ref[0, sl(h), :].astype(BF16) for h in hs]
    m_prev = [m_s[h][0:1, 0:1] for h in hs]
    ct = [c_s[h] for h in hs]
    n_row = [n_s[h][0:1, :] for h in hs]
    b_row = [brow_all[N_HEADS + h:N_HEADS + h + 1, :] for h in hs]
    g_col = [gcol[:, h:h + 1] - bcol_all[:, N_HEADS + h:N_HEADS + h + 1] for h in hs]

    sT = [_dot(k[h].astype(BF16), qTb[h]) for h in hs]
    cq = [_dot(ct[h].astype(BF16), qTb[h]) for h in hs]
    nq = [_dot(jnp.broadcast_to(n_row[h], (8, HEAD_DIM)).astype(BF16), qTb[h])[0:1, :] for h in hs]
    dT = [jnp.where(triu, g_col[h] + b_row[h], -jnp.inf) for h in hs]
    a_row = [b_row[h] + m_prev[h] for h in hs]
    m_t = [jnp.maximum(a_row[h], jnp.max(dT[h], axis=0, keepdims=True)) for h in hs]
    pT = [sT[h] * jnp.exp(dT[h] - m_t[h]) for h in hs]
    w_inter = [jnp.exp(a_row[h] - m_t[h]) for h in hs]
    numT = [_dot(vTb[h], pT[h].astype(BF16)) + w_inter[h] * cq[h] for h in hs]
    den = [jnp.sum(pT[h], axis=0, keepdims=True) + w_inter[h] * nq[h] for h in hs]
    hT = [numT[h] / jnp.maximum(jnp.abs(den[h]), jnp.exp(-m_t[h])) for h in hs]
    ms = [jnp.mean(hT[h] * hT[h], axis=0, keepdims=True) for h in hs]
    for h in hs:
        yT = hT[h] * lax.rsqrt(ms[h] + NORM_EPS) * gm_ref[sl(h), :] * oT_ref[0, sl(h), :]
        h_ref[:, sl(h)] = yT.T

    m_new = [m_t[h][:, L - 1:L] for h in hs]
    b_last = [b_row[h][:, L - 1:L] for h in hs]
    decay = [jnp.exp(b_last[h] + m_prev[h] - m_new[h]) for h in hs]
    kw = [k[h] * jnp.exp(b_last[h] + g_col[h] - m_new[h]) for h in hs]
    c_new = [decay[h] * ct[h] + _dot(vTb[h], kw[h].astype(BF16)) for h in hs]
    for h in hs:
        c_s[h] = c_new[h]
        n_s[h] = jnp.broadcast_to(decay[h] * n_row[h] + jnp.sum(kw[h], axis=0, keepdims=True), (8, HEAD_DIM))
        m_s[h] = jnp.broadcast_to(m_new[h], (8, HEAD_DIM))

    @pl.when(step == pl.num_programs(0) - 1)
    def _():
        for h in range(N_HEADS):
            c_out_ref[h] = c_s[h].T
        n_out_ref[...] = n_s[...]
        m_out_ref[...] = m_s[...]


def _mlstm_prompt(qT, k, vT, oT, gcol, grow, g_mhead):
    t = k.shape[0]
    L = qT.shape[2]
    assert t % L == 0 and qT.shape[0] * L == t
    blk_bytes = 3 * _nbytes((WIDTH, L), F32) + 2 * _nbytes((L, WIDTH), F32) + _nbytes((L, GATE_LANES), F32)
    state = jax.ShapeDtypeStruct((N_HEADS, 8, HEAD_DIM), F32)
    outs = pl.pallas_call(
        _mlstm_prompt_kernel,
        grid=(t // L,),
        in_specs=[pl.BlockSpec((1, WIDTH, L), lambda i: (i, 0, 0)),
                  pl.BlockSpec((L, WIDTH), lambda i: (i, 0)),
                  pl.BlockSpec((1, WIDTH, L), lambda i: (i, 0, 0)),
                  pl.BlockSpec((1, WIDTH, L), lambda i: (i, 0, 0)),
                  pl.BlockSpec((L, GATE_LANES), lambda i: (i, 0)),
                  pl.BlockSpec((1, GATE_ROWS, L), lambda i: (i, 0, 0)),
                  _const_spec((WIDTH, 1))],
        out_specs=[pl.BlockSpec((L, WIDTH), lambda i: (i, 0)),
                   pl.BlockSpec((N_HEADS, HEAD_DIM, HEAD_DIM), lambda i: (0, 0, 0)),
                   pl.BlockSpec((N_HEADS, 8, HEAD_DIM), lambda i: (0, 0, 0)),
                   pl.BlockSpec((N_HEADS, 8, HEAD_DIM), lambda i: (0, 0, 0))],
        out_shape=[jax.ShapeDtypeStruct((t, WIDTH), F32),
                   jax.ShapeDtypeStruct((N_HEADS, HEAD_DIM, HEAD_DIM), F32), state, state],
        scratch_shapes=[pltpu.VMEM((N_HEADS, HEAD_DIM, HEAD_DIM), F32),
                        pltpu.VMEM((N_HEADS, 8, HEAD_DIM), F32),
                        pltpu.VMEM((N_HEADS, 8, HEAD_DIM), F32)],
        compiler_params=_params(("arbitrary",), 2 * blk_bytes + 16 * _nbytes((L, L), F32)),
        name="mlstm_prompt",
    )(qT, k, vT, oT, gcol, grow, g_mhead.reshape(WIDTH, 1))
    hm, c_fin, n_fin, m_fin = outs
    return hm, c_fin, n_fin[:, 0, :], m_fin[:, 0, 0]


def _mlstm_sample_kernel(q_ref, k_ref, kT_ref, v_ref, o_ref, gcol_ref, grow_ref, c0_ref, n0_ref, m0_ref,
                         gm_ref, h_ref, c_out_ref, n_out_ref, m_out_ref):
    L = q_ref.shape[1]
    r = lax.broadcasted_iota(jnp.int32, (L, L), 0)
    c = lax.broadcasted_iota(jnp.int32, (L, L), 1)
    causal = c <= r
    items = [(b, h) for b in range(q_ref.shape[0]) for h in range(N_HEADS)]
    sl = lambda h: slice(h * HEAD_DIM, (h + 1) * HEAD_DIM)
    each = lambda f: [f(b, h) for b, h in items]

    gcol = [gcol_ref[b] for b in range(q_ref.shape[0])]
    grow = [grow_ref[b] for b in range(q_ref.shape[0])]
    bcol = [jnp.dot(causal.astype(F32), g, precision=HIGHEST, preferred_element_type=F32) for g in gcol]
    brow = [jnp.dot(g, (r <= c).astype(F32), precision=HIGHEST, preferred_element_type=F32) for g in grow]

    q = each(lambda b, h: q_ref[b][:, sl(h)])
    k = each(lambda b, h: k_ref[b][:, sl(h)])
    v = each(lambda b, h: v_ref[b][:, sl(h)])
    c_old = each(lambda b, h: c0_ref[b, h])
    n_old = each(lambda b, h: n0_ref[b][h:h + 1, :])
    i_col = each(lambda b, h: gcol[b][:, h:h + 1])
    i_row = each(lambda b, h: grow[b][h:h + 1, :])
    b_col = each(lambda b, h: bcol[b][:, N_HEADS + h:N_HEADS + h + 1])
    b_row = each(lambda b, h: brow[b][N_HEADS + h:N_HEADS + h + 1, :])
    m_prev = each(lambda b, h: m0_ref[b][h:h + 1, 0:1])
    n = range(len(items))

    qk = [_dot_nt(q[j], k[j]) for j in n]
    qc = [_dot(q[j].astype(BF16), c_old[j].astype(BF16)) for j in n]
    qn = [jnp.sum(q[j] * n_old[j], axis=1, keepdims=True) for j in n]
    dm = [jnp.where(causal, b_col[j] - b_row[j] + i_row[j], -jnp.inf) for j in n]
    a_col = [b_col[j] + m_prev[j] for j in n]
    m_t = [jnp.maximum(a_col[j], jnp.max(dm[j], axis=1, keepdims=True)) for j in n]
    s = [qk[j] * jnp.exp(dm[j] - m_t[j]) for j in n]
    w_inter = [jnp.exp(a_col[j] - m_t[j]) for j in n]
    num = [_dot(s[j], v[j]) + w_inter[j] * qc[j] for j in n]
    den = [jnp.sum(s[j], axis=1, keepdims=True) + w_inter[j] * qn[j] for j in n]
    hh = [num[j] / jnp.maximum(jnp.abs(den[j]), jnp.exp(-m_t[j])) for j in n]
    ms = [jnp.mean(hh[j] * hh[j], axis=-1, keepdims=True) for j in n]
    for j, (b, h) in enumerate(items):
        h_ref[b, :, sl(h)] = hh[j] * lax.rsqrt(ms[j] + NORM_EPS) * gm_ref[:, sl(h)] * o_ref[b][:, sl(h)]

    m_new = [m_t[j][L - 1:L, :] for j in n]
    b_last = [b_col[j][L - 1:L, :] for j in n]
    decay = [jnp.exp(b_last[j] + m_prev[j] - m_new[j]) for j in n]
    ws_row = [jnp.exp(b_last[j] - b_row[j] + i_row[j] - m_new[j]) for j in n]
    ws_col = [jnp.exp(b_last[j] - b_col[j] + i_col[j] - m_new[j]) for j in n]
    kv = [_dot(kT_ref[b][sl(h), :] * ws_row[j], v[j]) for j, (b, h) in enumerate(items)]
    for j, (b, h) in enumerate(items):
        c_out_ref[b, h] = decay[j] * c_old[j] + kv[j]
        n_out_ref[b, h:h + 1, :] = decay[j] * n_old[j] + jnp.sum(k[j] * ws_col[j], axis=0, keepdims=True)
        m_out_ref[b, h:h + 1, :] = jnp.broadcast_to(m_new[j], (1, HEAD_DIM))


def _mlstm_sample(q, k, kT, v, o, gcol, grow, c0, n0, m0, g_mhead):
    db, L, _ = q.shape
    sb = SEQS_PER_STEP if db % SEQS_PER_STEP == 0 else 1
    seq = lambda *shape: pl.BlockSpec((sb,) + shape, lambda b: (b,) + (0,) * len(shape))
    m0_rep = jnp.broadcast_to(m0[:, :, None], (db, N_HEADS, HEAD_DIM))
    nbytes = sb * (2 * 2 * _nbytes((N_HEADS, HEAD_DIM, HEAD_DIM), F32) + 16 * _nbytes((8, WIDTH), F32))
    outs = pl.pallas_call(
        _mlstm_sample_kernel,
        grid=(db // sb,),
        in_specs=[seq(L, WIDTH), seq(L, WIDTH), seq(WIDTH, L), seq(L, WIDTH), seq(L, WIDTH),
                  seq(L, GATE_LANES), seq(GATE_ROWS, L),
                  seq(N_HEADS, HEAD_DIM, HEAD_DIM), seq(N_HEADS, HEAD_DIM), seq(N_HEADS, HEAD_DIM),
                  _const_spec((1, WIDTH))],
        out_specs=[seq(L, WIDTH), seq(N_HEADS, HEAD_DIM, HEAD_DIM), seq(N_HEADS, HEAD_DIM),
                   seq(N_HEADS, HEAD_DIM)],
        out_shape=[jax.ShapeDtypeStruct((db, L, WIDTH), F32),
                   jax.ShapeDtypeStruct((db, N_HEADS, HEAD_DIM, HEAD_DIM), F32),
                   jax.ShapeDtypeStruct((db, N_HEADS, HEAD_DIM), F32),
                   jax.ShapeDtypeStruct((db, N_HEADS, HEAD_DIM), F32)],
        compiler_params=_params(("parallel",), nbytes),
        name="mlstm_sample",
    )(q, k, kT, v, o, gcol, grow, c0, n0, m0_rep, g_mhead.reshape(1, WIDTH))
    hm, c_new, n_new, m_new = outs
    return hm, c_new, n_new, m_new[:, :, 0]


def _rel_bucket(dist):
    max_exact = REL_BUCKETS // 2
    n = jnp.maximum(dist, 0)
    nf = jnp.maximum(n, 1).astype(F32)
    large = max_exact + (jnp.log(nf / max_exact) / math.log(REL_MAX_DIST / max_exact)
                         * (REL_BUCKETS - max_exact)).astype(jnp.int32)
    large = jnp.minimum(large, REL_BUCKETS - 1)
    return jnp.where(n < max_exact, n, large)


def _bias_of_bucket(bucket, rb_ref, h):
    out = jnp.zeros(bucket.shape, F32)
    for b in range(REL_BUCKETS):
        out = jnp.where(bucket == b, rb_ref[b * N_HEADS + h], out)
    return out


def _rel_bias_from_dist(dist, rb_ref, h):
    return _bias_of_bucket(_rel_bucket(dist), rb_ref, h)


def _prompt_bias_kernel(rb_ref, o_ref):
    delta = pl.program_id(0)
    s = lax.broadcasted_iota(jnp.int32, (MOBA_BLOCK, MOBA_BLOCK), 0)
    t = lax.broadcasted_iota(jnp.int32, (MOBA_BLOCK, MOBA_BLOCK), 1)
    dist = delta * MOBA_BLOCK + t - s
    bucket = _rel_bucket(dist)
    for h in range(N_HEADS):
        bias = _bias_of_bucket(bucket, rb_ref, h) - rb_ref[(REL_BUCKETS - 1) * N_HEADS + h]
        o_ref[0, h] = jnp.where(dist >= 0, bias * LOG2E, NEG_INF)


def _near_blocks():
    return -(-(REL_SAT + MOBA_BLOCK - 1) // MOBA_BLOCK)


def _prompt_bias(rel_bias):
    nd = _near_blocks() + 1
    return pl.pallas_call(
        _prompt_bias_kernel,
        grid=(nd,),
        in_specs=[pl.BlockSpec(memory_space=pltpu.SMEM)],
        out_specs=pl.BlockSpec((1, N_HEADS, MOBA_BLOCK, MOBA_BLOCK), lambda d: (d, 0, 0, 0)),
        out_shape=jax.ShapeDtypeStruct((nd, N_HEADS, MOBA_BLOCK, MOBA_BLOCK), F32),
        compiler_params=_params(("parallel",), 4 * N_HEADS * _nbytes((MOBA_BLOCK, MOBA_BLOCK), F32)),
        name="prompt_bias",
    )(rel_bias.reshape(-1))


def _top_mask(sc, valid, ksel, axis):
    n = sc.shape[axis]
    iota = lax.broadcasted_iota(jnp.int32, sc.shape, axis)
    cur = jnp.where(valid, sc, NEG_INF)
    sel = jnp.zeros(sc.shape, F32)
    for _ in range(ksel):
        mx = jnp.max(cur, axis=axis, keepdims=True)
        idx = jnp.min(jnp.where(cur == mx, iota, n), axis=axis, keepdims=True)
        hit = iota == idx
        sel = jnp.where(hit, 1.0, sel)
        cur = jnp.where(hit, -jnp.inf, cur)
    return jnp.where((sel > 0.0) & valid, 0.0, NEG_INF)


def _mask_rows(nb):
    return -(-nb // 16) * 16


def _prompt_select_kernel(ksel, kmean_ref, qT_ref, o_ref):
    tq = qT_ref.shape[2]
    nb = kmean_ref.shape[0]
    nbp = o_ref.shape[2] - HEAD_DIM
    q0 = pl.program_id(0) * tq
    own = (q0 + lax.broadcasted_iota(jnp.int32, (nb, tq), 1)) // MOBA_BLOCK
    blk = lax.broadcasted_iota(jnp.int32, (nb, tq), 0)
    for h in range(N_HEADS):
        sl = slice(h * HEAD_DIM, (h + 1) * HEAD_DIM)
        q = qT_ref[0, sl, :]
        sc = jnp.dot(kmean_ref[:, sl], q, precision=HIGHEST, preferred_element_type=F32)
        mask = jnp.where(blk == own, 0.0, _top_mask(sc, blk < own, ksel, 0))
        o_ref[0, h, 0:HEAD_DIM, :] = (q * (HEAD_DIM ** -0.5 * LOG2E)).astype(BF16)
        o_ref[0, h, HEAD_DIM:HEAD_DIM + nb, :] = mask.astype(BF16)
        if nbp > nb:
            o_ref[0, h, HEAD_DIM + nb:, :] = jnp.zeros((nbp - nb, tq), BF16)


def _prompt_select(kmean, qT):
    nb = kmean.shape[0]
    nt, _, tq = qT.shape
    ksel = min(MOBA_TOPK, nb - 1)
    rows = HEAD_DIM + _mask_rows(nb)
    return pl.pallas_call(
        functools.partial(_prompt_select_kernel, ksel),
        grid=(nt,),
        in_specs=[_const_spec((nb, WIDTH)), pl.BlockSpec((1, WIDTH, tq), lambda i: (i, 0, 0))],
        out_specs=pl.BlockSpec((1, N_HEADS, rows, tq), lambda i: (i, 0, 0, 0)),
        out_shape=jax.ShapeDtypeStruct((nt, N_HEADS, rows, tq), BF16),
        compiler_params=_params(("parallel",), 4 * _nbytes((WIDTH, tq), F32)),
        name="prompt_select",
    )(kmean, qT)


def _moba_prompt_kernel(n_bias, n_steps, qi_ref, gi_ref, last_ref, qa_ref, k_ref, vT_ref, bias_ref, o_ref):
    g_blocks = KV_GROUP
    rows = g_blocks * MOBA_BLOCK
    tq = qa_ref.shape[3]

    def logits(step):
        i, g = qi_ref[step], gi_ref[step]
        start = pl.multiple_of(g * rows, rows)
        sT = _dot(k_ref[0, pl.ds(start, rows), :], qa_ref[i, 0])
        sT = jnp.concatenate(
            [sT[b * MOBA_BLOCK:(b + 1) * MOBA_BLOCK]
             + bias_ref[jnp.clip(i - (g * g_blocks + b), 0, n_bias - 1), 0] for b in range(g_blocks)], axis=0)
        return sT, jnp.max(sT, axis=0, keepdims=True)

    def body(step, carry):
        m, l, acc, sT, mx = carry
        nxt = logits(step + 1)
        i, g = qi_ref[step], gi_ref[step]
        m_new = jnp.maximum(m, mx)
        alpha = jnp.exp2(m - m_new)
        l, acc = alpha * l, alpha * acc
        for b in range(g_blocks):
            p = jnp.exp2(sT[b * MOBA_BLOCK:(b + 1) * MOBA_BLOCK] - m_new)
            l = l + jnp.sum(p, axis=0, keepdims=True)
            acc = acc + _dot(vT_ref[g * g_blocks + b], p.astype(BF16))
        o_ref[pl.ds(pl.multiple_of(i * MOBA_BLOCK, MOBA_BLOCK), MOBA_BLOCK), :] = (
            acc / jnp.where(l > 0.0, l, 1.0)).T
        last = last_ref[step] == 1
        return (jnp.where(last, NEG_INF, m_new), jnp.where(last, 0.0, l), jnp.where(last, 0.0, acc)) + nxt

    init = (jnp.full((1, tq), NEG_INF, F32), jnp.zeros((1, tq), F32), jnp.zeros((HEAD_DIM, tq), F32)) + logits(0)
    lax.fori_loop(0, n_steps, body, init, unroll=8)


def _triangle_schedule(nb):
    qi, gi, last = [], [], []
    for i in range(nb):
        n_groups = i // KV_GROUP + 1
        for g in range(n_groups):
            qi.append(i)
            gi.append(g)
            last.append(int(g == n_groups - 1))
    n_steps = len(qi)
    as_i32 = lambda a: jnp.asarray(a + a[-1:], jnp.int32)
    return n_steps, as_i32(qi), as_i32(gi), as_i32(last)


def _moba_prompt(qa, k_aug, vT_bf, bias):
    nb, _, ka, _ = qa.shape
    t = k_aug.shape[1]
    n_bias = bias.shape[0]
    assert t == nb * MOBA_BLOCK and nb % KV_GROUP == 0 and vT_bf.shape == (nb, WIDTH, MOBA_BLOCK)
    n_steps, qi, gi, last = _triangle_schedule(nb)
    once = dict(pipeline_mode=pl.Buffered(1))
    nbytes = (_nbytes((t, -(-ka // 128) * 128), BF16) + _nbytes((t, ka), BF16) + _nbytes((t, HEAD_DIM), BF16)
              + _nbytes((n_bias, MOBA_BLOCK, MOBA_BLOCK), F32) + 2 * _nbytes((t, HEAD_DIM), F32)
              + 12 * KV_GROUP * _nbytes((MOBA_BLOCK, MOBA_BLOCK), F32))
    return pl.pallas_call(
        functools.partial(_moba_prompt_kernel, n_bias, n_steps),
        grid_spec=pltpu.PrefetchScalarGridSpec(
            num_scalar_prefetch=3,
            grid=(N_HEADS,),
            in_specs=[pl.BlockSpec((nb, 1, ka, MOBA_BLOCK), lambda h, *_: (0, h, 0, 0), **once),
                      pl.BlockSpec((1, t, ka), lambda h, *_: (h, 0, 0), **once),
                      pl.BlockSpec((nb, HEAD_DIM, MOBA_BLOCK), lambda h, *_: (0, h, 0), **once),
                      pl.BlockSpec((n_bias, 1, MOBA_BLOCK, MOBA_BLOCK), lambda h, *_: (0, h, 0, 0), **once)],
            out_specs=pl.BlockSpec((t, HEAD_DIM), lambda h, *_: (0, h))),
        out_shape=jax.ShapeDtypeStruct((t, WIDTH), F32),
        compiler_params=_params(("arbitrary",), nbytes),
        name="moba_prompt",
    )(qi, gi, last, qa, k_aug, vT_bf, bias)


def _sample_bias_kernel(past_len, t_new, rb_ref, past_ref, page_ref, new_ref):
    page = pl.program_id(0)

    def table(n_cols, first_kpos, masked):
        shape = (N_HEADS * t_new, n_cols)
        r = lax.broadcasted_iota(jnp.int32, shape, 0)
        c = lax.broadcasted_iota(jnp.int32, shape, 1)
        tpos = past_len + r % t_new
        kpos = first_kpos + c // N_HEADS
        bucket = _rel_bucket(tpos - kpos)
        ok = (kpos >= (tpos // MOBA_BLOCK) * MOBA_BLOCK) & (kpos <= tpos) if masked else None
        out = jnp.full(shape, NEG_INF, F32)
        for h in range(N_HEADS):
            same = (r // t_new == h) & (c % N_HEADS == h)
            out = jnp.where(same if ok is None else same & ok, _bias_of_bucket(bucket, rb_ref, h) * LOG2E, out)
        return out

    for u in range(past_ref.shape[0]):
        past_ref[u] = table(PAGE_ROWS, (page * past_ref.shape[0] + u) * PAGE_SIZE, False)

    @pl.when(page == 0)
    def _():
        page_ref[...] = table(PAGE_ROWS, past_len - PAGE_SIZE, True)
        new_ref[...] = table(new_ref.shape[1], past_len, True)


def _sample_bias(rel_bias, past_len, t_new):
    rows = N_HEADS * t_new
    n_pages = past_len // PAGE_SIZE
    new_cols = t_new * N_HEADS
    pg = math.gcd(n_pages, 8)
    return pl.pallas_call(
        functools.partial(_sample_bias_kernel, past_len, t_new),
        grid=(n_pages // pg,),
        in_specs=[pl.BlockSpec(memory_space=pltpu.SMEM)],
        out_specs=[pl.BlockSpec((pg, rows, PAGE_ROWS), lambda p: (p, 0, 0)),
                   pl.BlockSpec((rows, PAGE_ROWS), lambda p: (0, 0)),
                   pl.BlockSpec((rows, new_cols), lambda p: (0, 0))],
        out_shape=[jax.ShapeDtypeStruct((n_pages, rows, PAGE_ROWS), F32),
                   jax.ShapeDtypeStruct((rows, PAGE_ROWS), F32),
                   jax.ShapeDtypeStruct((rows, new_cols), F32)],
        compiler_params=_params(("arbitrary",), 16 * _nbytes((rows, PAGE_ROWS), F32)),
        name="sample_bias",
    )(rel_bias.reshape(-1))


def _page_specs(pps, n_pages):
    def spec(u):
        return pl.BlockSpec((1, PAGE_ROWS, HEAD_DIM), lambda b, c, pt: (pt[b * n_pages + c * pps + u], 0, 0))
    return [spec(u) for u in range(pps)]


def _softmax_partial(logit_tiles, value_tiles):
    m = jnp.max(logit_tiles[0], axis=1, keepdims=True)
    for lg in logit_tiles[1:]:
        m = jnp.maximum(m, jnp.max(lg, axis=1, keepdims=True))
    l, o = 0.0, 0.0
    for lg, v in zip(logit_tiles, value_tiles):
        p = jnp.exp2(lg - m)
        l = l + jnp.sum(p, axis=1, keepdims=True)
        o = o + _dot(p.astype(BF16), v.astype(BF16))
    return m, l, o


def _moba_sample_kernel(pps, n_blocks, t_new, pt_ref, q_ref, knew_ref, vnew_ref,
                        bpast_ref, bpage_ref, bnew_ref, *refs):
    k_refs, v_refs = refs[:pps], refs[pps:2 * pps]
    o_ref = refs[2 * pps]
    sc_s, m_s, l_s, o_s = refs[2 * pps + 1:]
    del pt_ref
    c = pl.program_id(1)
    nch = pl.num_programs(1)
    ppb = MOBA_BLOCK // PAGE_SIZE
    bps = pps // ppb
    rows = N_HEADS * t_new
    fold = 8 // N_HEADS
    ksel = min(MOBA_TOPK, n_blocks)
    q = q_ref[0]
    qb = (q * (HEAD_DIM ** -0.5 * LOG2E)).astype(BF16)
    lanes = lambda a: jnp.broadcast_to(a, (rows, HEAD_DIM))

    pages = [k_refs[u][0] for u in range(pps)]
    tiles = [_dot_nt(qb, pages[u].astype(BF16)) + bpast_ref[c * pps + u] for u in range(pps)]
    for b in range(bps):
        n = c * bps + b
        ksum = sum(jnp.sum(pages[b * ppb + u].reshape(PAGE_ROWS // 8, 8, HEAD_DIM), axis=0)
                   for u in range(ppb))
        kmean = sum(ksum[f * N_HEADS:(f + 1) * N_HEADS] for f in range(fold)) / MOBA_BLOCK
        kmean_rows = jnp.concatenate(
            [jnp.broadcast_to(kmean[h:h + 1], (t_new, HEAD_DIM)) for h in range(N_HEADS)], axis=0)
        sc_s[n] = lanes(jnp.sum(q * kmean_rows, axis=1, keepdims=True))
        m, l, o = _softmax_partial(tiles[b * ppb:(b + 1) * ppb], [v_refs[b * ppb + u][0] for u in range(ppb)])
        m_s[n], l_s[n], o_s[n] = lanes(m), lanes(l), o

    @pl.when(c == nch - 1)
    def _():
        own_page = _dot_nt(qb, pages[pps - 1].astype(BF16)) + bpage_ref[...]
        m_pg, l_pg, o_pg = _softmax_partial([own_page], [v_refs[pps - 1][0]])
        new = _dot_nt(qb, knew_ref[0].astype(BF16)) + bnew_ref[...]
        m_nw, l_nw, o_nw = _softmax_partial([new], [vnew_ref[0]])

        lane = lax.broadcasted_iota(jnp.int32, (rows, HEAD_DIM), 1)
        r = lax.broadcasted_iota(jnp.int32, (rows, HEAD_DIM), 0)
        sc, mm, ll = jnp.zeros((rows, HEAD_DIM), F32), jnp.zeros((rows, HEAD_DIM), F32), jnp.zeros((rows, HEAD_DIM), F32)
        for n in range(n_blocks):
            at = lane == n
            sc, mm, ll = jnp.where(at, sc_s[n], sc), jnp.where(at, m_s[n], mm), jnp.where(at, l_s[n], ll)
        tpos = n_blocks * MOBA_BLOCK + r % t_new
        valid = (lane < tpos // MOBA_BLOCK) & (lane < n_blocks)
        mm = jnp.where(lane < n_blocks, mm + _top_mask(sc, valid, ksel, 1), NEG_INF)
        m_all = jnp.maximum(jnp.max(mm, axis=1, keepdims=True), jnp.maximum(m_pg, m_nw))
        w = jnp.exp2(mm - m_all)
        w_pg, w_nw = jnp.exp2(m_pg - m_all), jnp.exp2(m_nw - m_all)
        den = jnp.sum(w * ll, axis=1, keepdims=True) + w_pg * l_pg + w_nw * l_nw
        num = w_pg * o_pg + w_nw * o_nw
        for n in range(n_blocks):
            num = num + jnp.sum(jnp.where(lane == n, w, 0.0), axis=1, keepdims=True) * o_s[n]
        o_ref[0] = num / den


def _moba_sample(q_rows, k_new, v_new, cache_k, cache_v, page_table, bias_past, bias_page, bias_new):
    db, rows, _ = q_rows.shape
    t_new = rows // N_HEADS
    n_pages = page_table.shape[1]
    ppb = MOBA_BLOCK // PAGE_SIZE
    assert n_pages % ppb == 0 and 8 % N_HEADS == 0 and t_new % 8 == 0
    n_blocks = n_pages // ppb
    assert n_blocks <= HEAD_DIM, "per-block scalars are gathered into one lane tile"
    pps = min(PAGES_PER_STEP, n_pages)
    assert n_pages % pps == 0 and pps % ppb == 0
    seq = lambda *shape: pl.BlockSpec((1,) + shape, lambda b, c, pt: (b,) + (0,) * len(shape))
    slab = pltpu.VMEM((n_blocks, rows, HEAD_DIM), F32)
    nbytes = (4 * pps * _nbytes((PAGE_ROWS, HEAD_DIM), F32) + _nbytes(bias_past.shape, F32)
              + 4 * _nbytes((n_blocks, rows, HEAD_DIM), F32) + 64 * _nbytes((rows, PAGE_ROWS), F32))
    return pl.pallas_call(
        functools.partial(_moba_sample_kernel, pps, n_blocks, t_new),
        grid_spec=pltpu.PrefetchScalarGridSpec(
            num_scalar_prefetch=1,
            grid=(db, n_pages // pps),
            in_specs=[seq(rows, HEAD_DIM), seq(rows, HEAD_DIM), seq(rows, HEAD_DIM),
                      _const_spec(bias_past.shape), _const_spec(bias_page.shape), _const_spec(bias_new.shape)]
                     + _page_specs(pps, n_pages) + _page_specs(pps, n_pages),
            out_specs=seq(rows, HEAD_DIM),
            scratch_shapes=[slab, slab, slab, slab]),
        out_shape=jax.ShapeDtypeStruct((db, rows, HEAD_DIM), F32),
        compiler_params=_params(("parallel", "arbitrary"), nbytes),
        name="moba_sample",
    )(page_table.reshape(-1), q_rows, k_new, v_new, bias_past, bias_page, bias_new,
      *([cache_k] * pps), *([cache_v] * pps))


def _out_ffn_kernel(ff_chunk, x_ref, hm_ref, ha_ref, sgm_ref, sga_ref, gate1_ref, shift2_ref, scale2_ref,
                    gate2_ref, g_ref, wbm_ref, wba_ref, wout_ref, wup_ref, wdn_ref, y_ref):
    bm = _dot(hm_ref[...].astype(BF16), wbm_ref[...])
    ba = _dot(ha_ref[...].astype(BF16), wba_ref[...])
    mix = sgm_ref[...] * bm + sga_ref[...] * ba
    x1 = x_ref[...] + gate1_ref[...] * _dot(mix.astype(BF16), wout_ref[...])
    y = x1 * lax.rsqrt(jnp.mean(x1 * x1, axis=-1, keepdims=True) + NORM_EPS) * g_ref[...]
    h2 = (y * (1.0 + scale2_ref[...]) + shift2_ref[...]).astype(BF16)
    d_ff = wup_ref.shape[1]
    acc = jnp.zeros(x1.shape, F32)
    for c in range(d_ff // ff_chunk):
        u = jnp.maximum(_dot(h2, wup_ref[:, c * ff_chunk:(c + 1) * ff_chunk]), 0.0)
        acc = acc + _dot((u * u).astype(BF16), wdn_ref[c * ff_chunk:(c + 1) * ff_chunk, :])
    y_ref[...] = x1 + gate2_ref[...] * acc


def _out_ffn(x, hm, ha, sgm, sga, gate1, shift2, scale2, gate2, g_norm, wbm, wba, wout, wup, wdn, tm):
    t, d = x.shape
    per_tok = gate1.shape[0] != 1
    tok = lambda n: pl.BlockSpec((tm, n), lambda i: (i, 0))
    mod = tok(d) if per_tok else pl.BlockSpec((1, d), lambda i: (0, 0))
    weights = (wbm, wba, wout, wup, wdn)
    nbytes = (sum(_nbytes(w.shape, w.dtype) for w in weights)
              + 2 * _nbytes((tm, d), F32) * (5 + (4 if per_tok else 0)) + 8 * _nbytes((tm, d), F32))
    return pl.pallas_call(
        functools.partial(_out_ffn_kernel, 1024),
        grid=(t // tm,),
        in_specs=[tok(d), tok(WIDTH), tok(WIDTH), tok(d), tok(d), mod, mod, mod, mod, _const_spec((1, d))]
                 + [_const_spec(w.shape) for w in weights],
        out_specs=tok(d),
        out_shape=jax.ShapeDtypeStruct((t, d), F32),
        compiler_params=_params(("parallel",), nbytes),
        name="out_ffn",
    )(x, hm, ha, sgm, sga, gate1, shift2, scale2, gate2, g_norm.reshape(1, d), *weights)


def _layer(x_prompt, x_sample, cache_k, cache_v, state_c, state_n, state_m, page_table, c_prompt, c_sample,
           rel_bias, w_ada, b_ada, g_norm_mix, w_in, b_igate, b_fgate, g_mhead, g_qnorm, g_knorm,
           w_branch_m, w_branch_a, w_out, g_norm_ffn, w_ff_up, w_ff_down):
    bsz, seq, d = x_prompt.shape
    db, t_new, _ = x_sample.shape
    assert bsz == 1, "the prompt path handles one sequence"
    n_pages = page_table.shape[1]
    past_len = n_pages * PAGE_SIZE
    k_scale = HEAD_DIM ** -0.5

    c_all = jnp.concatenate([c_prompt, c_sample], axis=0)
    pad = (-c_all.shape[0]) % 8
    mod = _ada(jnp.pad(c_all, ((0, pad), (0, 0))), w_ada, b_ada)
    mod_p = jnp.split(mod[0:1], N_COND, axis=-1)
    mod_s = jnp.split(jnp.repeat(mod[1:1 + db], t_new, axis=0), N_COND, axis=-1)

    cols = _split_w_in(w_in)
    gate_col, gate_row = _gate_groups(cols, b_igate, b_fgate)
    gq_row, gk_row = g_qnorm.reshape(1, WIDTH), g_knorm.reshape(1, WIDTH)
    gq_col = g_qnorm.reshape(WIDTH, 1)
    wbm, wba, wout = w_branch_m.astype(BF16), w_branch_a.astype(BF16), w_out.astype(BF16)
    wup, wdn = w_ff_up.astype(BF16), w_ff_down.astype(BF16)

    plan_p = [
        _group(cols["mq"], True), _group(cols["mk"], False, "scale", const=k_scale), _group(cols["mv"], True),
        _group(cols["mo"], True, "sigmoid"), gate_col, gate_row,
        _group(cols["aq"], True, "norm_col", auxv=gq_col),
        _group(cols["ak"], False, "norm_row", auxv=gk_row, outs=("heads", BF16, "mean")),
        _group(cols["av"], False, outs=("heads",)), _group(cols["av"], True, outs=(BF16,)),
        _group(cols["gm"], False, "sigmoid"), _group(cols["ga"], False, "sigmoid"),
    ]
    xp = x_prompt.reshape(seq, d)
    (mqT, mk, mvT, soT, gcol, grow, aqT, ak, ak_bf, kmean, av, avT_bf, sgm, sga) = _in_proj(
        xp, mod_p[0], mod_p[1], g_norm_mix, plan_p, TOKEN_TILE)
    hm, c_p, n_p, m_p = _mlstm_prompt(mqT, mk, mvT, soT, gcol, grow, g_mhead)
    nb = seq // MOBA_BLOCK
    qa = _prompt_select(kmean.reshape(nb, WIDTH), aqT)
    block_id = jnp.arange(seq, dtype=jnp.int32)[:, None] // MOBA_BLOCK
    one_hot = (block_id == jnp.arange(_mask_rows(nb), dtype=jnp.int32)[None, :]).astype(BF16)
    k_aug = jnp.concatenate([ak_bf.reshape(seq, N_HEADS, HEAD_DIM).transpose(1, 0, 2),
                             jnp.broadcast_to(one_hot, (N_HEADS,) + one_hot.shape)], axis=-1)
    ha = _moba_prompt(qa, k_aug, avT_bf, _prompt_bias(rel_bias))
    y_p = _out_ffn(xp, hm, ha, sgm, sga, mod_p[2], mod_p[3], mod_p[4], mod_p[5], g_norm_ffn,
                   wbm, wba, wout, wup, wdn, TOKEN_TILE)

    plan_s = [
        _group(cols["mq"], False), _group(cols["mk"], False, "scale", const=k_scale), _group(cols["mv"], False),
        _group(cols["mo"], False, "sigmoid"), gate_col, gate_row,
        _group(cols["aq"], False, "norm_row", auxv=gq_row),
        _group(cols["ak"], False, "norm_row", auxv=gk_row, outs=("heads",)),
        _group(cols["av"], False, outs=("heads",)),
        _group(cols["gm"], False, "sigmoid"), _group(cols["ga"], False, "sigmoid"),
    ]
    ts = db * t_new
    xs = x_sample.reshape(ts, d)
    (mq_s, mk_s, mv_s, so_s, gcol_s, grow_s, aq_s, ak_s, av_s, sgm_s, sga_s) = _in_proj(
        xs, mod_s[0], mod_s[1], g_norm_mix, plan_s, min(TOKEN_TILE, ts))
    per_seq = lambda a: a.reshape(db, t_new, a.shape[-1])
    mk_seq = per_seq(mk_s)
    grow_seq = jnp.transpose(grow_s, (1, 0, 2)).reshape(GATE_ROWS, db, t_new).transpose(1, 0, 2)
    hm_s, c_s, n_s, m_s = _mlstm_sample(per_seq(mq_s), mk_seq, jnp.swapaxes(mk_seq, 1, 2), per_seq(mv_s),
                                        per_seq(so_s), per_seq(gcol_s), grow_seq,
                                        state_c, state_n, state_m, g_mhead)
    rows = N_HEADS * t_new
    ck = cache_k.reshape(cache_k.shape[0], PAGE_ROWS, HEAD_DIM)
    cv = cache_v.reshape(cache_v.shape[0], PAGE_ROWS, HEAD_DIM)
    head_major = lambda a: a.reshape(db, t_new, N_HEADS, HEAD_DIM).transpose(0, 2, 1, 3).reshape(db, rows, HEAD_DIM)
    b_past, b_page, b_new = _sample_bias(rel_bias, past_len, t_new)
    ha_rows = _moba_sample(head_major(aq_s), ak_s.reshape(db, rows, HEAD_DIM), av_s.reshape(db, rows, HEAD_DIM),
                           ck, cv, page_table, b_past, b_page, b_new)
    ha_s = ha_rows.reshape(db, N_HEADS, t_new, HEAD_DIM).transpose(0, 2, 1, 3)
    y_s = _out_ffn(xs, hm_s.reshape(ts, WIDTH), ha_s.reshape(ts, WIDTH), sgm_s, sga_s,
                   mod_s[2], mod_s[3], mod_s[4], mod_s[5], g_norm_ffn, wbm, wba, wout, wup, wdn,
                   min(TOKEN_TILE, ts))

    heads = lambda a, lead: a.reshape(lead + (N_HEADS, HEAD_DIM))
    return (y_p.reshape(bsz, seq, d), y_s.reshape(db, t_new, d),
            heads(ak, (bsz, seq)), heads(av, (bsz, seq)), c_p[None], n_p[None], m_p[None],
            heads(ak_s, (db, t_new)), heads(av_s, (db, t_new)), c_s, n_s, m_s)


def kernel(x_prompt, x_sample, cache_k, cache_v, state_C, state_n, state_m, page_table, c_prompt, c_sample,
           rel_bias, w_ada, b_ada, g_norm_mix, w_in, b_igate, b_fgate, g_mhead, g_qnorm, g_knorm,
           w_branch_m, w_branch_a, w_out, g_norm_ffn, w_ff_up, w_ff_down):
    depth = w_in.shape[0]
    assert depth == 1, "state outputs are stacked per layer; this kernel implements the single-layer trunk"
    outs = _layer(x_prompt, x_sample, cache_k[0], cache_v[0], state_C[0], state_n[0], state_m[0], page_table,
                  c_prompt, c_sample, rel_bias, w_ada[0], b_ada[0], g_norm_mix[0], w_in[0], b_igate[0],
                  b_fgate[0], g_mhead[0], g_qnorm[0], g_knorm[0], w_branch_m[0], w_branch_a[0], w_out[0],
                  g_norm_ffn[0], w_ff_up[0], w_ff_down[0])
    yp, ys, kp, vp, cp, np_, mp, ks, vs, cs, ns, ms = outs
    stack = lambda a: a[None]
    return (yp, ys, stack(kp), stack(vp), stack(cp), stack(np_), stack(mp),
            stack(ks), stack(vs), stack(cs), stack(ns), stack(ms))
```
